```python
import math
import jax, jax.numpy as jnp
from jax import lax
import numpy as np

D_MODEL = 1024
BATCH = 8
SEQ = 8192
DEPTH = 2

CTX_LEN = 256
GRID_W = 64
HEAD_DIM = 64
ATTN_WIDTH = D_MODEL // 2
SGU_WIDTH = D_MODEL // 4
FOURIER_WIDTH = D_MODEL // 4
MIX_WIDTH = ATTN_WIDTH + SGU_WIDTH + FOURIER_WIDTH
N_Q_HEADS = ATTN_WIDTH // HEAD_DIM
N_KV_HEADS = N_Q_HEADS // 4
KV_WIDTH = N_KV_HEADS * HEAD_DIM
SGU_HEAD_DIM = 64
N_SGU_HEADS = SGU_WIDTH // SGU_HEAD_DIM
FOURIER_GROUP_DIM = 64
N_FOURIER_GROUPS = FOURIER_WIDTH // FOURIER_GROUP_DIM
IN_WIDTH = ATTN_WIDTH + 2 * KV_WIDTH + 2 * SGU_WIDTH + FOURIER_WIDTH
Q_END = ATTN_WIDTH
K_END = Q_END + KV_WIDTH
V_END = K_END + KV_WIDTH
U_END = V_END + SGU_WIDTH
G_END = U_END + SGU_WIDTH
WINDOW = 128
BLOCK = 128
CHUNK = 128
D_FF = 4 * D_MODEL
ROPE_THETA = 10000.0
EPS = 1e-6
NEG_INF = -1e30

kernel_name = "hymba_style_hybrid_dit_block"


def rmsnorm(x, g):
    xf = x.astype(jnp.float32)
    y = xf * lax.rsqrt(jnp.mean(xf * xf, axis=-1, keepdims=True) + EPS)
    return (y * g.astype(jnp.float32)).astype(x.dtype)


def axial_rope_tables(rows):
    row = jnp.repeat(jnp.arange(rows), GRID_W).astype(jnp.float32)
    col = jnp.tile(jnp.arange(GRID_W), rows).astype(jnp.float32)
    n_freq = HEAD_DIM // 4
    inv_freq = ROPE_THETA ** (-jnp.arange(n_freq, dtype=jnp.float32) / n_freq)
    ang_r = row[:, None] * inv_freq[None, :]
    ang_c = col[:, None] * inv_freq[None, :]
    ang = jnp.concatenate([ang_r, ang_r, ang_c, ang_c], axis=-1)
    return jnp.cos(ang), jnp.sin(ang)


def apply_rope(x, cos, sin):
    x1, x2, x3, x4 = jnp.split(x, 4, axis=-1)
    rot = jnp.concatenate([-x2, x1, -x4, x3], axis=-1)
    out = x * cos[None, :, None, :] + rot * sin[None, :, None, :]
    return out.astype(x.dtype)


def neighbour_blocks(t):
    z = jnp.zeros_like(t[:, :1])
    prev = jnp.concatenate([z, t[:, :-1]], axis=1)
    nxt = jnp.concatenate([t[:, 1:], z], axis=1)
    return jnp.concatenate([prev, t, nxt], axis=2)


def latent_window_attention(q, k, v, ck, cv, sink):
    B, S, H, hd = q.shape
    G = H // N_KV_HEADS
    nb = S // BLOCK
    scale = hd ** -0.5
    qb = q.reshape(B, nb, BLOCK, N_KV_HEADS, G, hd)
    kw = neighbour_blocks(k.reshape(B, nb, BLOCK, N_KV_HEADS, hd))
    vw = neighbour_blocks(v.reshape(B, nb, BLOCK, N_KV_HEADS, hd))
    s_loc = jnp.einsum('bnqhgd,bnkhd->bnhgqk', qb, kw, preferred_element_type=jnp.float32) * scale
    qpos = jnp.arange(BLOCK)[:, None]
    kpos = jnp.arange(3 * BLOCK)[None, :] - BLOCK
    band = jnp.abs(kpos - qpos) <= WINDOW
    kabs = jnp.arange(nb)[:, None, None] * BLOCK + kpos[None]
    valid = band[None] & (kabs >= 0) & (kabs < S)
    s_loc = jnp.where(valid[None, :, None, None], s_loc, NEG_INF)
    s_ctx = jnp.einsum('bnqhgd,bchd->bnhgqc', qb, ck, preferred_element_type=jnp.float32) * scale
    s_sink = jnp.broadcast_to(sink.astype(jnp.float32).reshape(1, 1, N_KV_HEADS, G, 1, 1),
                              s_loc.shape[:-1] + (1,))
    p = jax.nn.softmax(jnp.concatenate([s_loc, s_ctx, s_sink], axis=-1), axis=-1)
    n_loc = 3 * BLOCK
    n_ctx = ck.shape[1]
    p_loc = p[..., :n_loc].astype(v.dtype)
    p_ctx = p[..., n_loc:n_loc + n_ctx].astype(v.dtype)
    out = (jnp.einsum('bnhgqk,bnkhd->bnqhgd', p_loc, vw)
           + jnp.einsum('bnhgqc,bchd->bnqhgd', p_ctx, cv))
    return out.reshape(B, S, H * hd)


def context_attention(cq, ck, cv, sink):
    B, L, H, hd = cq.shape
    G = H // N_KV_HEADS
    qg = cq.reshape(B, L, N_KV_HEADS, G, hd)
    s = jnp.einsum('bqhgd,bkhd->bhgqk', qg, ck, preferred_element_type=jnp.float32) * (hd ** -0.5)
    s_sink = jnp.broadcast_to(sink.astype(jnp.float32).reshape(1, N_KV_HEADS, G, 1, 1),
                              s.shape[:-1] + (1,))
    p = jax.nn.softmax(jnp.concatenate([s, s_sink], axis=-1), axis=-1)[..., :L]
    out = jnp.einsum('bhgqk,bkhd->bqhgd', p.astype(cv.dtype), cv)
    return out.reshape(B, L, H * hd)


def spatial_gating(u, v, w_s, b_s, g_s):
    B, S, _ = u.shape
    vh = rmsnorm(v.reshape(B, S, N_SGU_HEADS, SGU_HEAD_DIM), g_s)
    vc = vh.reshape(B, S // CHUNK, CHUNK, N_SGU_HEADS, SGU_HEAD_DIM)
    mixed = jnp.einsum('hpq,bnqhd->bnphd', w_s, vc) + b_s.T[None, None, :, :, None]
    return u * mixed.reshape(B, S, SGU_WIDTH)


def fourier_mix(f):
    B, S, _ = f.shape
    fg = f.reshape(B, S, N_FOURIER_GROUPS, FOURIER_GROUP_DIM).astype(jnp.float32)
    y = jnp.real(jnp.fft.fft2(fg, axes=(1, 3), norm='ortho'))
    return y.reshape(B, S, FOURIER_WIDTH).astype(f.dtype)


def merge_heads(attn, sgu, four, g_mix, w_out):
    a = rmsnorm(attn, g_mix[:ATTN_WIDTH])
    s = rmsnorm(sgu, g_mix[ATTN_WIDTH:ATTN_WIDTH + SGU_WIDTH])
    f = rmsnorm(four, g_mix[ATTN_WIDTH + SGU_WIDTH:])
    return jnp.concatenate([a, s, f], axis=-1) @ w_out


def squared_relu_mlp(h, w1, w2):
    return jnp.square(jax.nn.relu(h @ w1)) @ w2


def setup_inputs(seed: int = 0) -> dict:
    key = jax.random.key(seed)
    ks = jax.random.split(key, 24)
    f32 = jnp.float32
    nrm = lambda k, shape, s: jax.random.normal(k, shape, f32) * s
    gain = lambda k, shape: 1.0 + 0.05 * jax.random.normal(k, shape, f32)
    return {
        "x": nrm(ks[0], (BATCH, SEQ, D_MODEL), 1.0),
        "c": nrm(ks[1], (BATCH, D_MODEL), 1.0),
        "ctx": nrm(ks[2], (BATCH, CTX_LEN, D_MODEL), 1.0),
        "c_ctx": nrm(ks[3], (D_MODEL,), 1.0),
        "w_ada": nrm(ks[4], (DEPTH, D_MODEL, 6 * D_MODEL), 0.5 * D_MODEL ** -0.5),
        "b_ada": nrm(ks[5], (DEPTH, 6 * D_MODEL), 0.02),
        "g_pre_mix": gain(ks[6], (DEPTH, D_MODEL)),
        "w_in": nrm(ks[7], (DEPTH, D_MODEL, IN_WIDTH), D_MODEL ** -0.5),
        "sink": nrm(ks[8], (DEPTH, N_Q_HEADS), 0.5),
        "w_sgu": nrm(ks[9], (DEPTH, N_SGU_HEADS, CHUNK, CHUNK), 0.5 * CHUNK ** -0.5),
        "b_sgu": gain(ks[10], (DEPTH, N_SGU_HEADS, CHUNK)),
        "g_sgu": gain(ks[11], (DEPTH, N_SGU_HEADS, SGU_HEAD_DIM)),
        "g_mix": gain(ks[12], (DEPTH, MIX_WIDTH)),
        "w_out": nrm(ks[13], (DEPTH, MIX_WIDTH, D_MODEL), MIX_WIDTH ** -0.5),
        "g_post_mix": gain(ks[14], (DEPTH, D_MODEL)),
        "g_pre_ff": gain(ks[15], (DEPTH, D_MODEL)),
        "w_ff1": nrm(ks[16], (DEPTH, D_MODEL, D_FF), D_MODEL ** -0.5),
        "w_ff2": nrm(ks[17], (DEPTH, D_FF, D_MODEL), D_FF ** -0.5),
        "g_post_ff": gain(ks[18], (DEPTH, D_MODEL)),
    }


def reference(x, c, ctx, c_ctx, w_ada, b_ada, g_pre_mix, w_in, sink, w_sgu, b_sgu, g_sgu,
              g_mix, w_out, g_post_mix, g_pre_ff, w_ff1, w_ff2, g_post_ff):
    B, S, _ = x.shape
    ROWS = S // GRID_W
    cos, sin = axial_rope_tables(ROWS)
    h_ctx = ctx
    for l in range(DEPTH):
        last = l == DEPTH - 1
        mod_x = (jax.nn.silu(c) @ w_ada[l] + b_ada[l])[:, None, :]
        sh1, sc1, g1, sh2, sc2, g2 = jnp.split(mod_x, 6, axis=-1)
        mod_c = jax.nn.silu(c_ctx) @ w_ada[l] + b_ada[l]
        csh1, csc1, cg1, csh2, csc2, cg2 = jnp.split(mod_c, 6, axis=-1)

        hc = rmsnorm(h_ctx, g_pre_mix[l]) * (1.0 + csc1) + csh1
        if last:
            ckv = hc @ w_in[l][:, Q_END:V_END]
            ck, cv = jnp.split(ckv, 2, axis=-1)
        else:
            pc = hc @ w_in[l]
            cq, ck, cv, cu, cg, cf = jnp.split(pc, [Q_END, K_END, V_END, U_END, G_END], axis=-1)
        ck = ck.reshape(B, CTX_LEN, N_KV_HEADS, HEAD_DIM)
        cv = cv.reshape(B, CTX_LEN, N_KV_HEADS, HEAD_DIM)
        if not last:
            c_attn = context_attention(cq.reshape(B, CTX_LEN, N_Q_HEADS, HEAD_DIM), ck, cv, sink[l])
            c_sgu = spatial_gating(jax.nn.gelu(cu), jax.nn.gelu(cg), w_sgu[l], b_sgu[l], g_sgu[l])
            c_four = fourier_mix(cf)
            yc = merge_heads(c_attn, c_sgu, c_four, g_mix[l], w_out[l])
            h_ctx = h_ctx + cg1 * rmsnorm(yc, g_post_mix[l])
            hc2 = rmsnorm(h_ctx, g_pre_ff[l]) * (1.0 + csc2) + csh2
            h_ctx = h_ctx + cg2 * rmsnorm(squared_relu_mlp(hc2, w_ff1[l], w_ff2[l]), g_post_ff[l])

        hx = rmsnorm(x, g_pre_mix[l]) * (1.0 + sc1) + sh1
        px = hx @ w_in[l]
        q, k, v, u, gv, f = jnp.split(px, [Q_END, K_END, V_END, U_END, G_END], axis=-1)
        q = apply_rope(q.reshape(B, S, N_Q_HEADS, HEAD_DIM), cos, sin)
        k = apply_rope(k.reshape(B, S, N_KV_HEADS, HEAD_DIM), cos, sin)
        v = v.reshape(B, S, N_KV_HEADS, HEAD_DIM)
        attn = latent_window_attention(q, k, v, ck, cv, sink[l])
        sgu = spatial_gating(jax.nn.gelu(u), jax.nn.gelu(gv), w_sgu[l], b_sgu[l], g_sgu[l])
        four = fourier_mix(f)
        y = merge_heads(attn, sgu, four, g_mix[l], w_out[l])
        x = x + g1 * rmsnorm(y, g_post_mix[l])
        hx2 = rmsnorm(x, g_pre_ff[l]) * (1.0 + sc2) + sh2
        x = x + g2 * rmsnorm(squared_relu_mlp(hx2, w_ff1[l], w_ff2[l]), g_post_ff[l])
    return x
```

```python
import functools
import math

import numpy as np
import jax
import jax.numpy as jnp
from jax import lax
from jax.experimental import pallas as pl
from jax.experimental.pallas import tpu as pltpu

F32 = jnp.float32
BF16 = jnp.bfloat16

HEAD_DIM = 64
GRID_W = 64
BLOCK = 128
CHUNK = 128
GROUP_DIM = 64
ROPE_THETA = 10000.0
EPS = 1e-6
NEG_INF = -1e30

V7X_LANES = 128
V7X_VMEM_BYTES = 64 * 2**20
VMEM_LIMIT = V7X_VMEM_BYTES * 3 // 4


def _params(*sem):
    return pltpu.CompilerParams(dimension_semantics=sem, vmem_limit_bytes=VMEM_LIMIT)


def _resident(shape):
    return pl.BlockSpec(shape, lambda *_: (0,) * len(shape), pipeline_mode=pl.Buffered(1))


def _rms(x):
    return x * lax.rsqrt(jnp.mean(x * x, axis=-1, keepdims=True) + EPS)


def _gelu_tanh(x):
    return x * (0.5 * (1.0 + jnp.tanh(math.sqrt(2.0 / math.pi) * (x + 0.044715 * (x * x * x)))))


def _dot(a, b):
    return jnp.dot(a, b, preferred_element_type=F32)


def _dot_nt(a, b):
    return lax.dot_general(a, b, (((1,), (1,)), ((), ())), preferred_element_type=F32)


def _mod_kernel(cc_ref, w_ref, b_ref, o_ref):
    cc = cc_ref[...]
    s = cc / (1.0 + jnp.exp(-cc))
    o_ref[0] = _dot(s.astype(BF16), w_ref[0].astype(BF16)) + b_ref[0]


def _modulation(cc, w_ada, b_ada):
    depth, d, n = w_ada.shape
    rows = cc.shape[0]
    tn = n // 4
    return pl.pallas_call(
        _mod_kernel,
        grid=(depth, n // tn),
        in_specs=[
            pl.BlockSpec((rows, d), lambda l, j: (0, 0)),
            pl.BlockSpec((1, d, tn), lambda l, j: (l, 0, j)),
            pl.BlockSpec((1, 1, tn), lambda l, j: (l, 0, j)),
        ],
        out_specs=pl.BlockSpec((1, rows, tn), lambda l, j: (l, 0, j)),
        out_shape=jax.ShapeDtypeStruct((depth, rows, n), F32),
        compiler_params=_params("arbitrary", "arbitrary"),
        name="modulation",
    )(cc, w_ada, b_ada.reshape(depth, 1, n))


def _in_proj_kernel(x_ref, sc_ref, sh_ref, g_ref, w_ref, cos_ref, sa_ref, sb_ref,
                    wsgu_ref, bsgu_ref, gsgu_ref, ones_ref,
                    q_ref, kd_ref, v_ref, sgu_ref, f_ref, *, widths):
    aw, kvw, sw, fw = widths
    tm = x_ref.shape[1]
    hx = _rms(x_ref[0]) * g_ref[...] * (1.0 + sc_ref[0]) + sh_ref[0]
    px = _dot(hx.astype(BF16), w_ref[...])

    cos, sa, sb = cos_ref[...], sa_ref[...], sb_ref[...]

    def rope(t):
        return t * cos + pltpu.roll(t, V7X_LANES - 16, 1) * sa + pltpu.roll(t, 16, 1) * sb

    scale = HEAD_DIM ** -0.5
    for j in range(aw // V7X_LANES):
        qj = rope(px[:, j * V7X_LANES:(j + 1) * V7X_LANES]) * scale
        q_ref[0, :, j * V7X_LANES:(j + 1) * V7X_LANES] = qj.astype(BF16)

    k = rope(px[:, aw:aw + kvw])
    kr = pltpu.roll(k, HEAD_DIM, 1)
    lo = lax.broadcasted_iota(jnp.int32, k.shape, 1) < HEAD_DIM
    kd_ref[0, :, :V7X_LANES] = jnp.where(lo, k, kr).astype(BF16)
    kd_ref[0, :, V7X_LANES:] = jnp.where(lo, kr, k).astype(BF16)
    v_ref[0] = px[:, aw + kvw:aw + 2 * kvw].astype(BF16)

    o = aw + 2 * kvw
    u = _gelu_tanh(px[:, o:o + sw])
    gv = _gelu_tanh(px[:, o + sw:o + 2 * sw])
    ss = gv * gv
    hi = ss.astype(BF16)
    lo2 = (ss - hi.astype(F32)).astype(BF16)
    gsum = _dot(hi, ones_ref[...]) + _dot(lo2, ones_ref[...])
    vn = (gv * lax.rsqrt(gsum * (1.0 / GROUP_DIM) + EPS) * gsgu_ref[...]).astype(BF16)
    lane = lax.broadcasted_iota(jnp.int32, (CHUNK, sw), 1)
    n_heads = sw // GROUP_DIM
    for c in range(tm // CHUNK):
        rows = slice(c * CHUNK, (c + 1) * CHUNK)
        r = _dot(wsgu_ref[...], vn[rows])
        mixed = r[(n_heads - 1) * CHUNK:]
        for h in range(n_heads - 2, -1, -1):
            mixed = jnp.where(lane < (h + 1) * GROUP_DIM, r[h * CHUNK:(h + 1) * CHUNK], mixed)
        sgu_ref[0, rows, :] = u[rows] * (mixed + bsgu_ref[...])

    f_ref[0] = px[:, o + 2 * sw:].astype(BF16)


def _in_proj(x, sc, sh, g, w_in, rope_tabs, wsgu, bsgu, gsgu, ones, tm, widths):
    b, t, d = x.shape
    aw, kvw, sw, fw = widths
    n_in = w_in.shape[1]
    tok = lambda w: pl.BlockSpec((1, tm, w), lambda bi, i: (bi, i, 0))
    vec = pl.BlockSpec((1, 1, d), lambda bi, i: (bi, 0, 0))
    tab = pl.BlockSpec((tm, V7X_LANES), lambda bi, i: (i, 0))
    return pl.pallas_call(
        functools.partial(_in_proj_kernel, widths=widths),
        grid=(b, t // tm),
        in_specs=[tok(d), vec, vec, _resident((1, d)), _resident((d, n_in)), tab, tab, tab,
                  _resident(wsgu.shape), _resident(bsgu.shape), _resident(gsgu.shape),
                  _resident(ones.shape)],
        out_specs=[tok(aw), tok(2 * kvw), tok(kvw), tok(sw), tok(fw)],
        out_shape=[jax.ShapeDtypeStruct((b, t, aw), BF16),
                   jax.ShapeDtypeStruct((b, t, 2 * kvw), BF16),
                   jax.ShapeDtypeStruct((b, t, kvw), BF16),
                   jax.ShapeDtypeStruct((b, t, sw), F32),
                   jax.ShapeDtypeStruct((b, t, fw), BF16)],
        compiler_params=_params("arbitrary", "arbitrary"),
        name="in_proj",
    )(x, sc, sh, g, w_in, *rope_tabs, wsgu, bsgu, gsgu, ones)


def _attn_kernel(sink_ref, q_ref, *refs, local, seq_len):
    if local:
        kp_ref, ko_ref, kn_ref, vp_ref, vo_ref, vn_ref, ck_ref, cv_ref, o_ref = refs
    else:
        ck_ref, cv_ref, o_ref = refs
    tq = q_ref.shape[1]
    n_kv = cv_ref.shape[2] // HEAD_DIM
    n_q = q_ref.shape[2] // HEAD_DIM
    grp = n_q // n_kv
    base = pl.program_id(1) * tq
    lane = lax.broadcasted_iota(jnp.int32, (BLOCK, V7X_LANES), 1)
    if local:
        v_all = jnp.concatenate([vp_ref[0], vo_ref[0], vn_ref[0]], axis=0)
        row = lax.broadcasted_iota(jnp.int32, (grp * BLOCK, 3 * BLOCK), 0)
        col = lax.broadcasted_iota(jnp.int32, (grp * BLOCK, 3 * BLOCK), 1)
        rel = col - BLOCK - (row & (BLOCK - 1))
        band = jnp.abs(rel) <= BLOCK
    cv = cv_ref[0]

    for sb in range(tq // BLOCK):
        qrows = slice(sb * BLOCK, (sb + 1) * BLOCK)
        outs = []
        for h in range(n_kv):
            kcols = slice(h * V7X_LANES, (h + 1) * V7X_LANES)
            parts = []
            for g in range(grp):
                head = h * grp + g
                qc = q_ref[0, qrows, (head // 2) * V7X_LANES:(head // 2 + 1) * V7X_LANES]
                keep = (lane < HEAD_DIM) if head % 2 == 0 else (lane >= HEAD_DIM)
                parts.append(jnp.where(keep, qc, jnp.zeros_like(qc)))
            qs = jnp.concatenate(parts, axis=0)
            sk = jnp.concatenate(
                [jnp.full((BLOCK, 1), sink_ref[h * grp + g], F32) for g in range(grp)], axis=0)
            s_ctx = _dot_nt(qs, ck_ref[0, :, kcols])
            m = jnp.maximum(jnp.max(s_ctx, axis=-1, keepdims=True), sk)
            if local:
                k_win = jnp.concatenate(
                    [kp_ref[0, :, kcols], ko_ref[0, :, kcols], kn_ref[0, :, kcols]],
                    axis=0)[sb * BLOCK:sb * BLOCK + 3 * BLOCK]
                s_loc = _dot_nt(qs, k_win)
                kabs = col + (base + (sb - 1) * BLOCK)
                s_loc = jnp.where(band, jnp.where(kabs >= 0, jnp.where(kabs < seq_len, s_loc, NEG_INF),
                                                  NEG_INF), NEG_INF)
                m = jnp.maximum(m, jnp.max(s_loc, axis=-1, keepdims=True))
            e_ctx = jnp.exp(s_ctx - m)
            den = jnp.sum(e_ctx, axis=-1, keepdims=True) + jnp.exp(sk - m)
            acc = _dot(e_ctx.astype(BF16), cv)
            if local:
                e_loc = jnp.exp(s_loc - m)
                den = den + jnp.sum(e_loc, axis=-1, keepdims=True)
                acc = acc + _dot(e_loc.astype(BF16), v_all[sb * BLOCK:sb * BLOCK + 3 * BLOCK])
            outs.append(acc / den)
        for g in range(grp):
            rows = slice(g * BLOCK, (g + 1) * BLOCK)
            o_ref[0, qrows, g * V7X_LANES:(g + 1) * V7X_LANES] = jnp.where(
                lane < HEAD_DIM, outs[0][rows], outs[1][rows]).astype(BF16)


def _attention(sink, q, kd, v, ckd, cv, tq, local):
    b, t, aw = q.shape
    ctx_len = cv.shape[1]
    kvw = cv.shape[2]
    nsb = tq // BLOCK
    nb = t // BLOCK
    smem = pl.BlockSpec(memory_space=pltpu.SMEM)
    qspec = pl.BlockSpec((1, tq, aw), lambda bi, i: (bi, i, 0))
    cspec = lambda w: pl.BlockSpec((1, ctx_len, w), lambda bi, i: (bi, 0, 0))
    in_specs = [smem, qspec]
    args = [sink, q]
    if local:
        prev = lambda w: pl.BlockSpec((1, BLOCK, w), lambda bi, i: (bi, jnp.maximum(i * nsb - 1, 0), 0))
        own = lambda w: pl.BlockSpec((1, tq, w), lambda bi, i: (bi, i, 0))
        nxt = lambda w: pl.BlockSpec((1, BLOCK, w), lambda bi, i: (bi, jnp.minimum((i + 1) * nsb, nb - 1), 0))
        in_specs += [prev(2 * kvw), own(2 * kvw), nxt(2 * kvw), prev(kvw), own(kvw), nxt(kvw)]
        args += [kd, kd, kd, v, v, v]
    in_specs += [cspec(2 * kvw), cspec(kvw)]
    args += [ckd, cv]
    return pl.pallas_call(
        functools.partial(_attn_kernel, local=local, seq_len=t),
        grid=(b, t // tq),
        in_specs=in_specs,
        out_specs=qspec,
        out_shape=jax.ShapeDtypeStruct((b, t, aw), BF16),
        compiler_params=_params("arbitrary", "arbitrary"),
        name="attention_local" if local else "attention_ctx",
    )(*args)


def _mxu_const(a):
    return jnp.asarray(a, F32).astype(BF16)


def _dft_tables(n_seq, n2):
    n1 = n_seq // n2
    s2 = np.arange(n2)[:, None, None]
    k1 = np.arange(n1)[None, :, None]
    s1 = np.arange(n1)[None, None, :]
    ang = 2.0 * np.pi * ((k1 * (n2 * s1 + s2)) % n_seq) / n_seq
    m1 = np.concatenate([np.cos(ang), -np.sin(ang)], axis=1)
    a2 = 2.0 * np.pi * ((np.arange(n2)[:, None] * np.arange(n2)[None, :]) % n2) / n2
    c2, s2m = np.cos(a2), np.sin(a2)
    w3 = np.block([[c2, s2m], [-s2m, c2]])
    return _mxu_const(m1), _mxu_const(w3)


def _channel_dft(width):
    d = np.arange(GROUP_DIM)
    a = 2.0 * np.pi * ((d[:, None] * d[None, :]) % GROUP_DIM) / GROUP_DIM
    eye = np.eye(width // GROUP_DIM)
    return np.concatenate([np.kron(eye, np.cos(a)), np.kron(eye, np.sin(a))], axis=0)


def _fft1_kernel(x_ref, m_ref, o_ref):
    o_ref[0] = lax.dot_general(m_ref[...], x_ref[0], (((2,), (1,)), ((0,), (0,))),
                               preferred_element_type=F32).astype(BF16)


def _fft2_kernel(x_ref, w_ref, cs_ref, o_ref, *, scale):
    n2 = o_ref.shape[2]
    for t in range(x_ref.shape[1]):
        z = _dot(w_ref[...], x_ref[0, t])
        zc = jnp.concatenate([z[:n2], z[n2:]], axis=1).astype(BF16)
        o_ref[0, t] = _dot(zc, cs_ref[...]) * scale


def _fourier_latent(f, m1, w3, cs):
    b, s, fw = f.shape
    n2 = w3.shape[0] // 2
    n1 = s // n2
    t2, t1 = 16, 8
    ft = f.reshape(b, n1, n2, fw).transpose(0, 2, 1, 3)
    a = pl.pallas_call(
        _fft1_kernel,
        grid=(b, n2 // t2),
        in_specs=[pl.BlockSpec((1, t2, n1, fw), lambda bi, i: (bi, i, 0, 0)),
                  pl.BlockSpec((t2, 2 * n1, n1), lambda bi, i: (i, 0, 0))],
        out_specs=pl.BlockSpec((1, t2, 2 * n1, fw), lambda bi, i: (bi, i, 0, 0)),
        out_shape=jax.ShapeDtypeStruct((b, n2, 2 * n1, fw), BF16),
        compiler_params=_params("arbitrary", "arbitrary"),
        name="fft_stage1",
    )(ft, m1)
    at = a.reshape(b, n2, 2, n1, fw).transpose(0, 3, 2, 1, 4).reshape(b, n1, 2 * n2, fw)
    y = pl.pallas_call(
        functools.partial(_fft2_kernel, scale=1.0 / math.sqrt(s * GROUP_DIM)),
        grid=(b, n1 // t1),
        in_specs=[pl.BlockSpec((1, t1, 2 * n2, fw), lambda bi, i: (bi, i, 0, 0)),
                  _resident(w3.shape), _resident(cs.shape)],
        out_specs=pl.BlockSpec((1, t1, n2, fw), lambda bi, i: (bi, i, 0, 0)),
        out_shape=jax.ShapeDtypeStruct((b, n1, n2, fw), F32),
        compiler_params=_params("arbitrary", "arbitrary"),
        name="fft_stage2",
    )(at, w3, cs)
    return y.transpose(0, 2, 1, 3).reshape(b, s, fw)


def _fourier_ctx_kernel(f_ref, cs_ref, dl_ref, o_ref, *, scale):
    fw = f_ref.shape[2]
    fc = _dot(f_ref[0], cs_ref[...])
    stacked = jnp.concatenate([fc[:, :fw], fc[:, fw:]], axis=0).astype(BF16)
    o_ref[0] = _dot(dl_ref[...], stacked) * scale


def _fourier_ctx(f, cs_t, dl):
    b, l, fw = f.shape
    return pl.pallas_call(
        functools.partial(_fourier_ctx_kernel, scale=1.0 / math.sqrt(l * GROUP_DIM)),
        grid=(b,),
        in_specs=[pl.BlockSpec((1, l, fw), lambda bi: (bi, 0, 0)),
                  _resident(cs_t.shape), _resident(dl.shape)],
        out_specs=pl.BlockSpec((1, l, fw), lambda bi: (bi, 0, 0)),
        out_shape=jax.ShapeDtypeStruct((b, l, fw), F32),
        compiler_params=_params("arbitrary"),
        name="fourier_ctx",
    )(f, cs_t, dl)


def _out_ff_kernel(x_ref, a_ref, s_ref, f_ref, g1_ref, sc2_ref, sh2_ref, g2_ref,
                   gmix_ref, gpm_ref, gpf_ref, gpo_ref, wo_ref, w1_ref, w2_ref, o_ref, *, ff_chunk):
    aw = a_ref.shape[2]
    sw = s_ref.shape[2]
    gm = gmix_ref[...]
    ycat = jnp.concatenate([
        (_rms(a_ref[0].astype(F32)) * gm[:, :aw]).astype(BF16),
        (_rms(s_ref[0]) * gm[:, aw:aw + sw]).astype(BF16),
        (_rms(f_ref[0]) * gm[:, aw + sw:]).astype(BF16)], axis=1)
    y = _dot(ycat, wo_ref[...])
    x1 = x_ref[0] + g1_ref[0] * (_rms(y) * gpm_ref[...])
    h2 = (_rms(x1) * gpf_ref[...] * (1.0 + sc2_ref[0]) + sh2_ref[0]).astype(BF16)
    d_ff = w1_ref.shape[1]
    ff = None
    for c in range(d_ff // ff_chunk):
        cols = slice(c * ff_chunk, (c + 1) * ff_chunk)
        hid = jnp.maximum(_dot(h2, w1_ref[:, cols]), 0.0)
        part = _dot((hid * hid).astype(BF16), w2_ref[cols, :])
        ff = part if ff is None else ff + part
    o_ref[0] = x1 + g2_ref[0] * (_rms(ff) * gpo_ref[...])


def _out_ff(x, a, sg, fo, g1, sc2, sh2, g2, gmix, gpm, gpf, gpo, wo, w1, w2, tm):
    b, t, d = x.shape
    tok = lambda w: pl.BlockSpec((1, tm, w), lambda bi, i: (bi, i, 0))
    vec = pl.BlockSpec((1, 1, d), lambda bi, i: (bi, 0, 0))
    return pl.pallas_call(
        functools.partial(_out_ff_kernel, ff_chunk=min(w1.shape[1], 1024)),
        grid=(b, t // tm),
        in_specs=[tok(d), tok(a.shape[2]), tok(sg.shape[2]), tok(fo.shape[2]), vec, vec, vec, vec,
                  _resident(gmix.shape), _resident(gpm.shape), _resident(gpf.shape), _resident(gpo.shape),
                  _resident(wo.shape), _resident(w1.shape), _resident(w2.shape)],
        out_specs=tok(d),
        out_shape=jax.ShapeDtypeStruct((b, t, d), F32),
        compiler_params=_params("arbitrary", "arbitrary"),
        name="out_ff",
    )(x, a, sg, fo, g1, sc2, sh2, g2, gmix, gpm, gpf, gpo, wo, w1, w2)


def _rope_tables(seq_len):
    rows = seq_len // GRID_W
    row = jnp.repeat(jnp.arange(rows), GRID_W).astype(F32)
    col = jnp.tile(jnp.arange(GRID_W), rows).astype(F32)
    n_freq = HEAD_DIM // 4
    inv_freq = ROPE_THETA ** (-jnp.arange(n_freq, dtype=F32) / n_freq)
    ang_r = row[:, None] * inv_freq[None, :]
    ang_c = col[:, None] * inv_freq[None, :]
    ang = jnp.concatenate([ang_r, ang_r, ang_c, ang_c], axis=-1)
    reps = V7X_LANES // HEAD_DIM
    cos = jnp.tile(jnp.cos(ang), (1, reps))
    sin = jnp.tile(jnp.sin(ang), (1, reps))
    even = (np.arange(V7X_LANES) // n_freq) % 2 == 0
    return cos, jnp.where(even, -sin, 0.0), jnp.where(even, 0.0, sin)


def kernel(x, c, ctx, c_ctx, w_ada, b_ada, g_pre_mix, w_in, sink, w_sgu, b_sgu, g_sgu,
           g_mix, w_out, g_post_mix, g_pre_ff, w_ff1, w_ff2, g_post_ff):
    b, s, d = x.shape
    ctx_len = ctx.shape[1]
    depth = w_ada.shape[0]
    n_q = sink.shape[1]
    aw = n_q * HEAD_DIM
    n_sgu = w_sgu.shape[1]
    sw = n_sgu * GROUP_DIM
    fw = g_mix.shape[1] - aw - sw
    kvw = (w_in.shape[2] - aw - 2 * sw - fw) // 2
    n_kv = kvw // HEAD_DIM
    grp = n_q // n_kv
    assert n_kv == 2 and 2 * kvw == 2 * V7X_LANES and s % (4 * BLOCK) == 0 and ctx_len % BLOCK == 0
    widths = (aw, kvw, sw, fw)

    rows = -(-(b + 1) // 8) * 8
    cc = jnp.zeros((rows, d), F32).at[:b].set(c).at[b].set(c_ctx)
    mod = _modulation(cc, w_ada, b_ada)

    rope_x = _rope_tables(s)
    rope_c = (jnp.ones((ctx_len, V7X_LANES), F32), jnp.zeros((ctx_len, V7X_LANES), F32),
              jnp.zeros((ctx_len, V7X_LANES), F32))
    ones = _mxu_const(np.kron(np.eye(n_sgu), np.ones((GROUP_DIM, GROUP_DIM))))
    m1, w3 = _dft_tables(s, BLOCK)
    cs_np = _channel_dft(fw)
    cs = _mxu_const(cs_np)
    cs_t = _mxu_const(np.concatenate([cs_np[:fw], cs_np[fw:]], axis=1))
    kl = np.arange(ctx_len)
    al = 2.0 * np.pi * ((kl[:, None] * kl[None, :]) % ctx_len) / ctx_len
    dl = _mxu_const(np.concatenate([np.cos(al), -np.sin(al)], axis=1))

    lane = np.arange(aw)
    perm = np.where(lane % V7X_LANES < HEAD_DIM,
                    (lane // V7X_LANES) * HEAD_DIM + lane % V7X_LANES,
                    (grp + lane // V7X_LANES) * HEAD_DIM + lane % V7X_LANES - HEAD_DIM)

    row2 = lambda v: v.reshape(1, -1)
    h_ctx = ctx
    for l in range(depth):
        last = l == depth - 1
        mx = [m.reshape(b, 1, d) for m in jnp.split(mod[l, :b], 6, axis=-1)]
        mc = [jnp.broadcast_to(m.reshape(1, 1, d), (b, 1, d)) for m in jnp.split(mod[l, b], 6, axis=-1)]
        w_in_b = w_in[l].astype(BF16)
        wsgu = w_sgu[l].reshape(n_sgu * CHUNK, CHUNK).astype(BF16)
        bsgu = jnp.repeat(b_sgu[l].T, GROUP_DIM, axis=1)
        gsgu = g_sgu[l].reshape(1, sw)
        gmix = row2(jnp.concatenate([g_mix[l][:aw][perm], g_mix[l][aw:]]))
        wo = jnp.concatenate([w_out[l][:aw][perm], w_out[l][aw:]], axis=0).astype(BF16)
        shared = (row2(g_pre_mix[l]), w_in_b)
        sgu_w = (wsgu, bsgu, gsgu, ones)
        ff_w = (gmix, row2(g_post_mix[l]), row2(g_pre_ff[l]), row2(g_post_ff[l]), wo,
                w_ff1[l].astype(BF16), w_ff2[l].astype(BF16))

        cq, ckd, cv, csg, cf = _in_proj(h_ctx, mc[1], mc[0], *shared, rope_c, *sgu_w, ctx_len, widths)
        if not last:
            ca = _attention(sink[l], cq, None, None, ckd, cv, ctx_len, local=False)
            cfo = _fourier_ctx(cf, cs_t, dl)
            h_ctx = _out_ff(h_ctx, ca, csg, cfo, mc[2], mc[4], mc[3], mc[5], *ff_w, ctx_len)

        q, kd, v, sg, f = _in_proj(x, mx[1], mx[0], *shared, rope_x, *sgu_w, 4 * BLOCK, widths)
        a = _attention(sink[l], q, kd, v, ckd, cv, 2 * BLOCK, local=True)
        fo = _fourier_latent(f, m1, w3, cs)
        x = _out_ff(x, a, sg, fo, mx[2], mx[4], mx[3], mx[5], *ff_w, 4 * BLOCK)
    return x
```

```python
import functools
import math

import numpy as np
import jax
import jax.numpy as jnp
from jax import lax
from jax.experimental import pallas as pl
from jax.experimental.pallas import tpu as pltpu

F32 = jnp.float32
BF16 = jnp.bfloat16

HEAD_DIM = 64
GRID_W = 64
BLOCK = 128
CHUNK = 128
GROUP_DIM = 64
ROPE_THETA = 10000.0
EPS = 1e-6
NEG_INF = -1e30

V7X_LANES = 128
V7X_VMEM_BYTES = 64 * 2**20
VMEM_LIMIT = V7X_VMEM_BYTES * 3 // 4


def _params(*sem):
    return pltpu.CompilerParams(dimension_semantics=sem, vmem_limit_bytes=VMEM_LIMIT)


def _resident(shape):
    return pl.BlockSpec(shape, lambda *_: (0,) * len(shape), pipeline_mode=pl.Buffered(1))


def _rms(x):
    return x * lax.rsqrt(jnp.mean(x * x, axis=-1, keepdims=True) + EPS)


def _gelu_tanh(x):
    return x * (0.5 * (1.0 + jnp.tanh(math.sqrt(2.0 / math.pi) * (x + 0.044715 * (x * x * x)))))


def _dot(a, b):
    return jnp.dot(a, b, preferred_element_type=F32)


def _dot_nt(a, b):
    return lax.dot_general(a, b, (((1,), (1,)), ((), ())), preferred_element_type=F32)


def _mod_kernel(cc_ref, w_ref, b_ref, o_ref):
    cc = cc_ref[...]
    s = cc / (1.0 + jnp.exp(-cc))
    o_ref[0] = _dot(s.astype(BF16), w_ref[0].astype(BF16)) + b_ref[0]


def _modulation(cc, w_ada, b_ada):
    depth, d, n = w_ada.shape
    rows = cc.shape[0]
    tn = n // 4
    return pl.pallas_call(
        _mod_kernel,
        grid=(depth, n // tn),
        in_specs=[
            pl.BlockSpec((rows, d), lambda l, j: (0, 0)),
            pl.BlockSpec((1, d, tn), lambda l, j: (l, 0, j)),
            pl.BlockSpec((1, 1, tn), lambda l, j: (l, 0, j)),
        ],
        out_specs=pl.BlockSpec((1, rows, tn), lambda l, j: (l, 0, j)),
        out_shape=jax.ShapeDtypeStruct((depth, rows, n), F32),
        compiler_params=_params("arbitrary", "arbitrary"),
        name="modulation",
    )(cc, w_ada, b_ada.reshape(depth, 1, n))


def _in_proj_kernel(x_ref, sc_ref, sh_ref, g_ref, w_ref, cos_ref, sa_ref, sb_ref,
                    wsgu_ref, bsgu_ref, gsgu_ref, ones_ref,
                    q_ref, kd_ref, v_ref, sgu_ref, f_ref, *, widths):
    aw, kvw, sw, fw = widths
    tm = x_ref.shape[1]
    hx = _rms(x_ref[0]) * g_ref[...] * (1.0 + sc_ref[0]) + sh_ref[0]
    px = _dot(hx.astype(BF16), w_ref[...])

    cos, sa, sb = cos_ref[...], sa_ref[...], sb_ref[...]

    def rope(t):
        return t * cos + pltpu.roll(t, V7X_LANES - 16, 1) * sa + pltpu.roll(t, 16, 1) * sb

    scale = HEAD_DIM ** -0.5
    for j in range(aw // V7X_LANES):
        qj = rope(px[:, j * V7X_LANES:(j + 1) * V7X_LANES]) * scale
        q_ref[0, :, j * V7X_LANES:(j + 1) * V7X_LANES] = qj.astype(BF16)

    k = rope(px[:, aw:aw + kvw])
    kr = pltpu.roll(k, HEAD_DIM, 1)
    lo = lax.broadcasted_iota(jnp.int32, k.shape, 1) < HEAD_DIM
    kd_ref[0, :, :V7X_LANES] = jnp.where(lo, k, kr).astype(BF16)
    kd_ref[0, :, V7X_LANES:] = jnp.where(lo, kr, k).astype(BF16)
    v_ref[0] = px[:, aw + kvw:aw + 2 * kvw].T.astype(BF16)

    o = aw + 2 * kvw
    u = _gelu_tanh(px[:, o:o + sw])
    gv = _gelu_tanh(px[:, o + sw:o + 2 * sw])
    ss = gv * gv
    hi = ss.astype(BF16)
    lo2 = (ss - hi.astype(F32)).astype(BF16)
    gsum = _dot(hi, ones_ref[...]) + _dot(lo2, ones_ref[...])
    vn = (gv * lax.rsqrt(gsum * (1.0 / GROUP_DIM) + EPS) * gsgu_ref[...]).astype(BF16)
    lane = lax.broadcasted_iota(jnp.int32, (CHUNK, sw), 1)
    n_heads = sw // GROUP_DIM
    for c in range(tm // CHUNK):
        rows = slice(c * CHUNK, (c + 1) * CHUNK)
        r = _dot(wsgu_ref[...], vn[rows])
        mixed = r[(n_heads - 1) * CHUNK:]
        for h in range(n_heads - 2, -1, -1):
            mixed = jnp.where(lane < (h + 1) * GROUP_DIM, r[h * CHUNK:(h + 1) * CHUNK], mixed)
        sgu_ref[0, rows, :] = u[rows] * (mixed + bsgu_ref[...])

    f_ref[0] = px[:, o + 2 * sw:].astype(BF16)


def _in_proj(x, sc, sh, g, w_in, rope_tabs, wsgu, bsgu, gsgu, ones, tm, widths):
    b, t, d = x.shape
    aw, kvw, sw, fw = widths
    n_in = w_in.shape[1]
    tok = lambda w: pl.BlockSpec((1, tm, w), lambda bi, i: (bi, i, 0))
    vec = pl.BlockSpec((1, 1, d), lambda bi, i: (bi, 0, 0))
    tab = pl.BlockSpec((tm, V7X_LANES), lambda bi, i: (i, 0))
    return pl.pallas_call(
        functools.partial(_in_proj_kernel, widths=widths),
        grid=(b, t // tm),
        in_specs=[tok(d), vec, vec, _resident((1, d)), _resident((d, n_in)), tab, tab, tab,
                  _resident(wsgu.shape), _resident(bsgu.shape), _resident(gsgu.shape),
                  _resident(ones.shape)],
        out_specs=[tok(aw), tok(2 * kvw), pl.BlockSpec((1, kvw, tm), lambda bi, i: (bi, 0, i)),
                   tok(sw), tok(fw)],
        out_shape=[jax.ShapeDtypeStruct((b, t, aw), BF16),
                   jax.ShapeDtypeStruct((b, t, 2 * kvw), BF16),
                   jax.ShapeDtypeStruct((b, kvw, t), BF16),
                   jax.ShapeDtypeStruct((b, t, sw), F32),
                   jax.ShapeDtypeStruct((b, t, fw), BF16)],
        compiler_params=_params("arbitrary", "arbitrary"),
        name="in_proj",
    )(x, sc, sh, g, w_in, *rope_tabs, wsgu, bsgu, gsgu, ones)


def _attn_kernel(sink_ref, q_ref, *refs, local):
    if local:
        kp_ref, ko_ref, kn_ref, vp_ref, vo_ref, vn_ref, ck_ref, cv_ref, o_ref = refs
    else:
        ck_ref, cv_ref, o_ref = refs
    tq = q_ref.shape[1]
    kvw = cv_ref.shape[1]
    n_kv = kvw // HEAD_DIM
    grp = q_ref.shape[2] // HEAD_DIM // n_kv
    cols = grp * BLOCK
    lane = lax.broadcasted_iota(jnp.int32, (BLOCK, V7X_LANES), 1)
    head_lane = lax.broadcasted_iota(jnp.int32, (1, cols), 1)
    kv_row = lax.broadcasted_iota(jnp.int32, (kvw, cols), 0)
    if local:
        vt_all = jnp.concatenate([vp_ref[0], vo_ref[0], vn_ref[0]], axis=1)
        key = lax.broadcasted_iota(jnp.int32, (BLOCK, cols), 0)
        qpos = lax.broadcasted_iota(jnp.int32, (BLOCK, cols), 1) & (BLOCK - 1)
        ok_prev = key >= qpos
        ok_next = key <= qpos
        not_first = pl.program_id(1) > 0
        not_last = pl.program_id(1) < pl.num_programs(1) - 1
    cvt = cv_ref[0]
    colmax = lambda t: jnp.max(t, axis=0, keepdims=True)
    colsum = lambda t: jnp.sum(t, axis=0, keepdims=True)

    n_sb = tq // BLOCK
    for sb in range(n_sb):
        qrows = slice(sb * BLOCK, (sb + 1) * BLOCK)
        win = slice(sb * BLOCK, (sb + 3) * BLOCK)
        outs = []
        for h in range(n_kv):
            kcols = slice(h * V7X_LANES, (h + 1) * V7X_LANES)
            parts = []
            for g in range(grp):
                head = h * grp + g
                qc = q_ref[0, qrows, (head // 2) * V7X_LANES:(head // 2 + 1) * V7X_LANES]
                keep = (lane < HEAD_DIM) if head % 2 == 0 else (lane >= HEAD_DIM)
                parts.append(jnp.where(keep, qc, jnp.zeros_like(qc)))
            qs = jnp.concatenate(parts, axis=0)
            sk = jnp.full((1, cols), sink_ref[h * grp + grp - 1], F32)
            for g in range(grp - 2, -1, -1):
                sk = jnp.where(head_lane < (g + 1) * BLOCK, sink_ref[h * grp + g], sk)
            s_ctx = _dot_nt(ck_ref[0, :, kcols], qs)
            m = jnp.maximum(colmax(s_ctx), sk)
            if local:
                k_win = jnp.concatenate(
                    [kp_ref[0, :, kcols], ko_ref[0, :, kcols], kn_ref[0, :, kcols]], axis=0)[win]
                s_loc = _dot_nt(k_win, qs)
                s_prev = jnp.where(ok_prev, s_loc[:BLOCK], NEG_INF)
                s_own = s_loc[BLOCK:2 * BLOCK]
                s_next = jnp.where(ok_next, s_loc[2 * BLOCK:], NEG_INF)
                if sb == 0:
                    s_prev = jnp.where(not_first, s_prev, NEG_INF)
                if sb == n_sb - 1:
                    s_next = jnp.where(not_last, s_next, NEG_INF)
                m = jnp.maximum(jnp.maximum(m, colmax(s_own)),
                                jnp.maximum(colmax(s_prev), colmax(s_next)))
            e_ctx = jnp.exp(s_ctx - m)
            den = colsum(e_ctx) + jnp.exp(sk - m)
            acc = _dot(cvt, e_ctx.astype(BF16))
            if local:
                e_loc = jnp.concatenate(
                    [jnp.exp(s_prev - m), jnp.exp(s_own - m), jnp.exp(s_next - m)], axis=0)
                den = den + colsum(e_loc)
                acc = acc + _dot(vt_all[:, win], e_loc.astype(BF16))
            outs.append(acc / den)
        o_t = jnp.where(kv_row < HEAD_DIM, outs[0], outs[1])
        for g in range(grp):
            o_ref[0, qrows, g * V7X_LANES:(g + 1) * V7X_LANES] = (
                o_t[:, g * BLOCK:(g + 1) * BLOCK].T.astype(BF16))


def _attention(sink, q, kd, vt, ckd, cvt, tq, local):
    b, t, aw = q.shape
    kvw, ctx_len = cvt.shape[1:]
    nsb = tq // BLOCK
    nb = t // BLOCK
    smem = pl.BlockSpec(memory_space=pltpu.SMEM)
    qspec = pl.BlockSpec((1, tq, aw), lambda bi, i: (bi, i, 0))
    in_specs = [smem, qspec]
    args = [sink, q]
    if local:
        prev = lambda bi, i: (bi, jnp.maximum(i * nsb - 1, 0))
        nxt = lambda bi, i: (bi, jnp.minimum((i + 1) * nsb, nb - 1))
        in_specs += [pl.BlockSpec((1, BLOCK, 2 * kvw), lambda bi, i: (*prev(bi, i), 0)),
                     pl.BlockSpec((1, tq, 2 * kvw), lambda bi, i: (bi, i, 0)),
                     pl.BlockSpec((1, BLOCK, 2 * kvw), lambda bi, i: (*nxt(bi, i), 0)),
                     pl.BlockSpec((1, kvw, BLOCK), lambda bi, i: (bi, 0, prev(bi, i)[1])),
                     pl.BlockSpec((1, kvw, tq), lambda bi, i: (bi, 0, i)),
                     pl.BlockSpec((1, kvw, BLOCK), lambda bi, i: (bi, 0, nxt(bi, i)[1]))]
        args += [kd, kd, kd, vt, vt, vt]
    in_specs += [pl.BlockSpec((1, ctx_len, 2 * kvw), lambda bi, i: (bi, 0, 0)),
                 pl.BlockSpec((1, kvw, ctx_len), lambda bi, i: (bi, 0, 0))]
    args += [ckd, cvt]
    return pl.pallas_call(
        functools.partial(_attn_kernel, local=local),
        grid=(b, t // tq),
        in_specs=in_specs,
        out_specs=qspec,
        out_shape=jax.ShapeDtypeStruct((b, t, aw), BF16),
        compiler_params=_params("arbitrary", "arbitrary"),
        name="attention_local" if local else "attention_ctx",
    )(*args)


def _mxu_const(a):
    return jnp.asarray(a, F32).astype(BF16)


def _dft_tables(n_seq, n2):
    n1 = n_seq // n2
    s2 = np.arange(n2)[:, None, None]
    k1 = np.arange(n1)[None, :, None]
    s1 = np.arange(n1)[None, None, :]
    ang = 2.0 * np.pi * ((k1 * (n2 * s1 + s2)) % n_seq) / n_seq
    m1 = np.concatenate([np.cos(ang), -np.sin(ang)], axis=1)
    a2 = 2.0 * np.pi * ((np.arange(n2)[:, None] * np.arange(n2)[None, :]) % n2) / n2
    c2, s2m = np.cos(a2), np.sin(a2)
    w3 = np.block([[c2, s2m], [-s2m, c2]])
    return _mxu_const(m1), _mxu_const(w3)


def _channel_dft(width):
    d = np.arange(GROUP_DIM)
    a = 2.0 * np.pi * ((d[:, None] * d[None, :]) % GROUP_DIM) / GROUP_DIM
    eye = np.eye(width // GROUP_DIM)
    return np.concatenate([np.kron(eye, np.cos(a)), np.kron(eye, np.sin(a))], axis=0)


def _fft1_kernel(x_ref, m_ref, o_ref):
    o_ref[0] = lax.dot_general(m_ref[...], x_ref[0], (((2,), (1,)), ((0,), (0,))),
                               preferred_element_type=F32).astype(BF16)


def _fft2_kernel(x_ref, w_ref, cs_ref, o_ref, *, scale):
    n2 = o_ref.shape[2]
    for t in range(x_ref.shape[1]):
        z = _dot(w_ref[...], x_ref[0, t])
        zc = jnp.concatenate([z[:n2], z[n2:]], axis=1).astype(BF16)
        o_ref[0, t] = _dot(zc, cs_ref[...]) * scale


def _fourier_latent(f, m1, w3, cs):
    b, s, fw = f.shape
    n2 = w3.shape[0] // 2
    n1 = s // n2
    t2, t1 = 16, 8
    ft = f.reshape(b, n1, n2, fw).transpose(0, 2, 1, 3)
    a = pl.pallas_call(
        _fft1_kernel,
        grid=(b, n2 // t2),
        in_specs=[pl.BlockSpec((1, t2, n1, fw), lambda bi, i: (bi, i, 0, 0)),
                  pl.BlockSpec((t2, 2 * n1, n1), lambda bi, i: (i, 0, 0))],
        out_specs=pl.BlockSpec((1, t2, 2 * n1, fw), lambda bi, i: (bi, i, 0, 0)),
        out_shape=jax.ShapeDtypeStruct((b, n2, 2 * n1, fw), BF16),
        compiler_params=_params("arbitrary", "arbitrary"),
        name="fft_stage1",
    )(ft, m1)
    at = a.reshape(b, n2, 2, n1, fw).transpose(0, 3, 2, 1, 4).reshape(b, n1, 2 * n2, fw)
    y = pl.pallas_call(
        functools.partial(_fft2_kernel, scale=1.0 / math.sqrt(s * GROUP_DIM)),
        grid=(b, n1 // t1),
        in_specs=[pl.BlockSpec((1, t1, 2 * n2, fw), lambda bi, i: (bi, i, 0, 0)),
                  _resident(w3.shape), _resident(cs.shape)],
        out_specs=pl.BlockSpec((1, t1, n2, fw), lambda bi, i: (bi, i, 0, 0)),
        out_shape=jax.ShapeDtypeStruct((b, n1, n2, fw), F32),
        compiler_params=_params("arbitrary", "arbitrary"),
        name="fft_stage2",
    )(at, w3, cs)
    return y.transpose(0, 2, 1, 3).reshape(b, s, fw)


def _fourier_ctx_kernel(f_ref, cs_ref, dl_ref, o_ref, *, scale):
    fw = f_ref.shape[2]
    fc = _dot(f_ref[0], cs_ref[...])
    stacked = jnp.concatenate([fc[:, :fw], fc[:, fw:]], axis=0).astype(BF16)
    o_ref[0] = _dot(dl_ref[...], stacked) * scale


def _fourier_ctx(f, cs_t, dl):
    b, l, fw = f.shape
    return pl.pallas_call(
        functools.partial(_fourier_ctx_kernel, scale=1.0 / math.sqrt(l * GROUP_DIM)),
        grid=(b,),
        in_specs=[pl.BlockSpec((1, l, fw), lambda bi: (bi, 0, 0)),
                  _resident(cs_t.shape), _resident(dl.shape)],
        out_specs=pl.BlockSpec((1, l, fw), lambda bi: (bi, 0, 0)),
        out_shape=jax.ShapeDtypeStruct((b, l, fw), F32),
        compiler_params=_params("arbitrary"),
        name="fourier_ctx",
    )(f, cs_t, dl)


def _out_ff_kernel(x_ref, a_ref, s_ref, f_ref, g1_ref, sc2_ref, sh2_ref, g2_ref,
                   gmix_ref, gpm_ref, gpf_ref, gpo_ref, wo_ref, w1_ref, w2_ref, o_ref, *, ff_chunk):
    aw = a_ref.shape[2]
    sw = s_ref.shape[2]
    gm = gmix_ref[...]
    ycat = jnp.concatenate([
        (_rms(a_ref[0].astype(F32)) * gm[:, :aw]).astype(BF16),
        (_rms(s_ref[0]) * gm[:, aw:aw + sw]).astype(BF16),
        (_rms(f_ref[0]) * gm[:, aw + sw:]).astype(BF16)], axis=1)
    y = _dot(ycat, wo_ref[...])
    x1 = x_ref[0] + g1_ref[0] * (_rms(y) * gpm_ref[...])
    h2 = (_rms(x1) * gpf_ref[...] * (1.0 + sc2_ref[0]) + sh2_ref[0]).astype(BF16)
    d_ff = w1_ref.shape[1]
    ff = None
    for c in range(d_ff // ff_chunk):
        cols = slice(c * ff_chunk, (c + 1) * ff_chunk)
        hid = jnp.maximum(_dot(h2, w1_ref[:, cols]), 0.0)
        part = _dot((hid * hid).astype(BF16), w2_ref[cols, :])
        ff = part if ff is None else ff + part
    o_ref[0] = x1 + g2_ref[0] * (_rms(ff) * gpo_ref[...])


def _out_ff(x, a, sg, fo, g1, sc2, sh2, g2, gmix, gpm, gpf, gpo, wo, w1, w2, tm):
    b, t, d = x.shape
    tok = lambda w: pl.BlockSpec((1, tm, w), lambda bi, i: (bi, i, 0))
    vec = pl.BlockSpec((1, 1, d), lambda bi, i: (bi, 0, 0))
    return pl.pallas_call(
        functools.partial(_out_ff_kernel, ff_chunk=min(w1.shape[1], 1024)),
        grid=(b, t // tm),
        in_specs=[tok(d), tok(a.shape[2]), tok(sg.shape[2]), tok(fo.shape[2]), vec, vec, vec, vec,
                  _resident(gmix.shape), _resident(gpm.shape), _resident(gpf.shape), _resident(gpo.shape),
                  _resident(wo.shape), _resident(w1.shape), _resident(w2.shape)],
        out_specs=tok(d),
        out_shape=jax.ShapeDtypeStruct((b, t, d), F32),
        compiler_params=_params("arbitrary", "arbitrary"),
        name="out_ff",
    )(x, a, sg, fo, g1, sc2, sh2, g2, gmix, gpm, gpf, gpo, wo, w1, w2)


def _rope_tables(seq_len):
    rows = seq_len // GRID_W
    row = jnp.repeat(jnp.arange(rows), GRID_W).astype(F32)
    col = jnp.tile(jnp.arange(GRID_W), rows).astype(F32)
    n_freq = HEAD_DIM // 4
    inv_freq = ROPE_THETA ** (-jnp.arange(n_freq, dtype=F32) / n_freq)
    ang_r = row[:, None] * inv_freq[None, :]
    ang_c = col[:, None] * inv_freq[None, :]
    ang = jnp.concatenate([ang_r, ang_r, ang_c, ang_c], axis=-1)
    reps = V7X_LANES // HEAD_DIM
    cos = jnp.tile(jnp.cos(ang), (1, reps))
    sin = jnp.tile(jnp.sin(ang), (1, reps))
    even = (np.arange(V7X_LANES) // n_freq) % 2 == 0
    return cos, jnp.where(even, -sin, 0.0), jnp.where(even, 0.0, sin)


def kernel(x, c, ctx, c_ctx, w_ada, b_ada, g_pre_mix, w_in, sink, w_sgu, b_sgu, g_sgu,
           g_mix, w_out, g_post_mix, g_pre_ff, w_ff1, w_ff2, g_post_ff):
    b, s, d = x.shape
    ctx_len = ctx.shape[1]
    depth = w_ada.shape[0]
    n_q = sink.shape[1]
    aw = n_q * HEAD_DIM
    n_sgu = w_sgu.shape[1]
    sw = n_sgu * GROUP_DIM
    fw = g_mix.shape[1] - aw - sw
    kvw = (w_in.shape[2] - aw - 2 * sw - fw) // 2
    n_kv = kvw // HEAD_DIM
    grp = n_q // n_kv
    assert n_kv == 2 and 2 * kvw == 2 * V7X_LANES and s % (4 * BLOCK) == 0 and ctx_len % BLOCK == 0
    widths = (aw, kvw, sw, fw)

    rows = -(-(b + 1) // 8) * 8
    cc = jnp.zeros((rows, d), F32).at[:b].set(c).at[b].set(c_ctx)
    mod = _modulation(cc, w_ada, b_ada)

    rope_x = _rope_tables(s)
    rope_c = (jnp.ones((ctx_len, V7X_LANES), F32), jnp.zeros((ctx_len, V7X_LANES), F32),
              jnp.zeros((ctx_len, V7X_LANES), F32))
    ones = _mxu_const(np.kron(np.eye(n_sgu), np.ones((GROUP_DIM, GROUP_DIM))))
    m1, w3 = _dft_tables(s, BLOCK)
    cs_np = _channel_dft(fw)
    cs = _mxu_const(cs_np)
    cs_t = _mxu_const(np.concatenate([cs_np[:fw], cs_np[fw:]], axis=1))
    kl = np.arange(ctx_len)
    al = 2.0 * np.pi * ((kl[:, None] * kl[None, :]) % ctx_len) / ctx_len
    dl = _mxu_const(np.concatenate([np.cos(al), -np.sin(al)], axis=1))

    lane = np.arange(aw)
    perm = np.where(lane % V7X_LANES < HEAD_DIM,
                    (lane // V7X_LANES) * HEAD_DIM + lane % V7X_LANES,
                    (grp + lane // V7X_LANES) * HEAD_DIM + lane % V7X_LANES - HEAD_DIM)

    row2 = lambda v: v.reshape(1, -1)
    h_ctx = ctx
    for l in range(depth):
        last = l == depth - 1
        mx = [m.reshape(b, 1, d) for m in jnp.split(mod[l, :b], 6, axis=-1)]
        mc = [jnp.broadcast_to(m.reshape(1, 1, d), (b, 1, d)) for m in jnp.split(mod[l, b], 6, axis=-1)]
        w_in_b = w_in[l].astype(BF16)
        wsgu = w_sgu[l].reshape(n_sgu * CHUNK, CHUNK).astype(BF16)
        bsgu = jnp.repeat(b_sgu[l].T, GROUP_DIM, axis=1)
        gsgu = g_sgu[l].reshape(1, sw)
        gmix = row2(jnp.concatenate([g_mix[l][:aw][perm], g_mix[l][aw:]]))
        wo = jnp.concatenate([w_out[l][:aw][perm], w_out[l][aw:]], axis=0).astype(BF16)
        shared = (row2(g_pre_mix[l]), w_in_b)
        sgu_w = (wsgu, bsgu, gsgu, ones)
        ff_w = (gmix, row2(g_post_mix[l]), row2(g_pre_ff[l]), row2(g_post_ff[l]), wo,
                w_ff1[l].astype(BF16), w_ff2[l].astype(BF16))

        cq, ckd, cvt, csg, cf = _in_proj(h_ctx, mc[1], mc[0], *shared, rope_c, *sgu_w, ctx_len, widths)
        if not last:
            ca = _attention(sink[l], cq, None, None, ckd, cvt, ctx_len, local=False)
            cfo = _fourier_ctx(cf, cs_t, dl)
            h_ctx = _out_ff(h_ctx, ca, csg, cfo, mc[2], mc[4], mc[3], mc[5], *ff_w, ctx_len)

        q, kd, vt, sg, f = _in_proj(x, mx[1], mx[0], *shared, rope_x, *sgu_w, 4 * BLOCK, widths)
        a = _attention(sink[l], q, kd, vt, ckd, cvt, 2 * BLOCK, local=True)
        fo = _fourier_latent(f, m1, w3, cs)
        x = _out_ff(x, a, sg, fo, mx[2], mx[4], mx[3], mx[5], *ff_w, 4 * BLOCK)
    return x
```

```python
import functools
import math

import numpy as np
import jax
import jax.numpy as jnp
from jax import lax
from jax.experimental import pallas as pl
from jax.experimental.pallas import tpu as pltpu

F32 = jnp.float32
BF16 = jnp.bfloat16

HEAD_DIM = 64
GRID_W = 64
BLOCK = 128
CHUNK = 128
GROUP_DIM = 64
ROPE_THETA = 10000.0
EPS = 1e-6
NEG_INF = -1e30
LOG2E = math.log2(math.e)
SCORE_LOOKAHEAD = 2

V7X_LANES = 128
V7X_VMEM_BYTES = 64 * 2**20
VMEM_LIMIT = V7X_VMEM_BYTES * 3 // 4


def _params(*sem):
    return pltpu.CompilerParams(dimension_semantics=sem, vmem_limit_bytes=VMEM_LIMIT)


def _resident(shape):
    return pl.BlockSpec(shape, lambda *_: (0,) * len(shape), pipeline_mode=pl.Buffered(1))


def _rms(x):
    return x * lax.rsqrt(jnp.mean(x * x, axis=-1, keepdims=True) + EPS)


def _gelu_tanh(x):
    return x * (0.5 * (1.0 + jnp.tanh(math.sqrt(2.0 / math.pi) * (x + 0.044715 * (x * x * x)))))


def _dot(a, b):
    return jnp.dot(a, b, preferred_element_type=F32)


def _dot_nt(a, b):
    return lax.dot_general(a, b, (((1,), (1,)), ((), ())), preferred_element_type=F32)


def _mod_kernel(cc_ref, w_ref, b_ref, o_ref):
    cc = cc_ref[...]
    s = cc / (1.0 + jnp.exp(-cc))
    o_ref[0] = _dot(s.astype(BF16), w_ref[0].astype(BF16)) + b_ref[0]


def _modulation(cc, w_ada, b_ada):
    depth, d, n = w_ada.shape
    rows = cc.shape[0]
    tn = n // 4
    return pl.pallas_call(
        _mod_kernel,
        grid=(depth, n // tn),
        in_specs=[
            pl.BlockSpec((rows, d), lambda l, j: (0, 0)),
            pl.BlockSpec((1, d, tn), lambda l, j: (l, 0, j)),
            pl.BlockSpec((1, 1, tn), lambda l, j: (l, 0, j)),
        ],
        out_specs=pl.BlockSpec((1, rows, tn), lambda l, j: (l, 0, j)),
        out_shape=jax.ShapeDtypeStruct((depth, rows, n), F32),
        compiler_params=_params("arbitrary", "arbitrary"),
        name="modulation",
    )(cc, w_ada, b_ada.reshape(depth, 1, n))


def _in_proj_kernel(x_ref, sc_ref, sh_ref, g_ref, w_ref, cos_ref, sa_ref, sb_ref,
                    wsgu_ref, bsgu_ref, gsgu_ref, ones_ref,
                    q_ref, kd_ref, v_ref, sgu_ref, f_ref, *, widths):
    aw, kvw, sw, fw = widths
    tm = x_ref.shape[1]
    hx = _rms(x_ref[0]) * g_ref[...] * (1.0 + sc_ref[0]) + sh_ref[0]
    px = _dot(hx.astype(BF16), w_ref[...])

    cos, sa, sb = cos_ref[...], sa_ref[...], sb_ref[...]

    def rope(t):
        return t * cos + pltpu.roll(t, V7X_LANES - 16, 1) * sa + pltpu.roll(t, 16, 1) * sb

    scale = HEAD_DIM ** -0.5 * LOG2E
    for j in range(aw // V7X_LANES):
        qj = rope(px[:, j * V7X_LANES:(j + 1) * V7X_LANES]) * scale
        q_ref[0, :, j * V7X_LANES:(j + 1) * V7X_LANES] = qj.astype(BF16)

    k = rope(px[:, aw:aw + kvw])
    kr = pltpu.roll(k, HEAD_DIM, 1)
    lo = lax.broadcasted_iota(jnp.int32, k.shape, 1) < HEAD_DIM
    kd_ref[0, :, :V7X_LANES] = jnp.where(lo, k, kr).astype(BF16)
    kd_ref[0, :, V7X_LANES:] = jnp.where(lo, kr, k).astype(BF16)
    vt = px[:, aw + kvw:aw + 2 * kvw].T
    head0 = lax.broadcasted_iota(jnp.int32, vt.shape, 0) < HEAD_DIM
    v_ref[0, :kvw, :] = jnp.where(head0, vt, 1.0).astype(BF16)
    v_ref[0, kvw:, :] = jnp.where(head0, 1.0, vt).astype(BF16)

    o = aw + 2 * kvw
    u = _gelu_tanh(px[:, o:o + sw])
    gv = _gelu_tanh(px[:, o + sw:o + 2 * sw])
    ss = gv * gv
    hi = ss.astype(BF16)
    lo2 = (ss - hi.astype(F32)).astype(BF16)
    gsum = _dot(hi, ones_ref[...]) + _dot(lo2, ones_ref[...])
    vn = (gv * lax.rsqrt(gsum * (1.0 / GROUP_DIM) + EPS) * gsgu_ref[...]).astype(BF16)
    lane = lax.broadcasted_iota(jnp.int32, (CHUNK, sw), 1)
    n_heads = sw // GROUP_DIM
    for c in range(tm // CHUNK):
        rows = slice(c * CHUNK, (c + 1) * CHUNK)
        r = _dot(wsgu_ref[...], vn[rows])
        mixed = r[(n_heads - 1) * CHUNK:]
        for h in range(n_heads - 2, -1, -1):
            mixed = jnp.where(lane < (h + 1) * GROUP_DIM, r[h * CHUNK:(h + 1) * CHUNK], mixed)
        sgu_ref[0, rows, :] = u[rows] * (mixed + bsgu_ref[...])

    f_ref[0] = px[:, o + 2 * sw:].astype(BF16)


def _in_proj(x, sc, sh, g, w_in, rope_tabs, wsgu, bsgu, gsgu, ones, tm, widths):
    b, t, d = x.shape
    aw, kvw, sw, fw = widths
    n_in = w_in.shape[1]
    tok = lambda w: pl.BlockSpec((1, tm, w), lambda bi, i: (bi, i, 0))
    vec = pl.BlockSpec((1, 1, d), lambda bi, i: (bi, 0, 0))
    tab = pl.BlockSpec((tm, V7X_LANES), lambda bi, i: (i, 0))
    return pl.pallas_call(
        functools.partial(_in_proj_kernel, widths=widths),
        grid=(b, t // tm),
        in_specs=[tok(d), vec, vec, _resident((1, d)), _resident((d, n_in)), tab, tab, tab,
                  _resident(wsgu.shape), _resident(bsgu.shape), _resident(gsgu.shape),
                  _resident(ones.shape)],
        out_specs=[tok(aw), tok(2 * kvw), pl.BlockSpec((1, 2 * kvw, tm), lambda bi, i: (bi, 0, i)),
                   tok(sw), tok(fw)],
        out_shape=[jax.ShapeDtypeStruct((b, t, aw), BF16),
                   jax.ShapeDtypeStruct((b, t, 2 * kvw), BF16),
                   jax.ShapeDtypeStruct((b, 2 * kvw, t), BF16),
                   jax.ShapeDtypeStruct((b, t, sw), F32),
                   jax.ShapeDtypeStruct((b, t, fw), BF16)],
        compiler_params=_params("arbitrary", "arbitrary"),
        name="in_proj",
    )(x, sc, sh, g, w_in, *rope_tabs, wsgu, bsgu, gsgu, ones)


def _attn_kernel(sink_ref, q_ref, *refs, local):
    if local:
        kp_ref, ko_ref, kn_ref, vp_ref, vo_ref, vn_ref, ck_ref, cv_ref, o_ref = refs
    else:
        ck_ref, cv_ref, o_ref = refs
    tq = q_ref.shape[1]
    kvw = ck_ref.shape[2] // 2
    n_kv = kvw // HEAD_DIM
    grp = q_ref.shape[2] // HEAD_DIM // n_kv
    cols = grp * BLOCK
    lane = lax.broadcasted_iota(jnp.int32, (BLOCK, V7X_LANES), 1)
    head_lane = lax.broadcasted_iota(jnp.int32, (1, cols), 1)
    kv_row = lax.broadcasted_iota(jnp.int32, (kvw, cols), 0)
    if local:
        vt_all = jnp.concatenate([vp_ref[0], vo_ref[0], vn_ref[0]], axis=1)
        key = lax.broadcasted_iota(jnp.int32, (BLOCK, cols), 0)
        qpos = lax.broadcasted_iota(jnp.int32, (BLOCK, cols), 1) & (BLOCK - 1)
        ok_prev = key >= qpos
        ok_next = key <= qpos
        not_first = pl.program_id(1) > 0
        not_last = pl.program_id(1) < pl.num_programs(1) - 1
    colmax = lambda t: jnp.max(t, axis=0, keepdims=True)

    n_sb = tq // BLOCK

    def scores(sb, h):
        qrows = slice(sb * BLOCK, (sb + 1) * BLOCK)
        kcols = slice(h * V7X_LANES, (h + 1) * V7X_LANES)
        parts = []
        for g in range(grp):
            head = h * grp + g
            qc = q_ref[0, qrows, (head // 2) * V7X_LANES:(head // 2 + 1) * V7X_LANES]
            keep = (lane < HEAD_DIM) if head % 2 == 0 else (lane >= HEAD_DIM)
            parts.append(jnp.where(keep, qc, jnp.zeros_like(qc)))
        qs = jnp.concatenate(parts, axis=0)
        tiles = [_dot_nt(ck_ref[0, :, kcols], qs)]
        if local:
            k_win = jnp.concatenate(
                [kp_ref[0, :, kcols], ko_ref[0, :, kcols], kn_ref[0, :, kcols]],
                axis=0)[sb * BLOCK:(sb + 3) * BLOCK]
            s_loc = _dot_nt(k_win, qs)
            s_prev = jnp.where(ok_prev, s_loc[:BLOCK], NEG_INF)
            s_next = jnp.where(ok_next, s_loc[2 * BLOCK:], NEG_INF)
            if sb == 0:
                s_prev = jnp.where(not_first, s_prev, NEG_INF)
            if sb == n_sb - 1:
                s_next = jnp.where(not_last, s_next, NEG_INF)
            tiles += [s_prev, s_loc[BLOCK:2 * BLOCK], s_next]
        return tiles

    def attend(sb, h, tiles):
        vrows = slice(h * kvw, (h + 1) * kvw)
        ones_row = ((h + 1) % n_kv) * HEAD_DIM
        sk = jnp.full((1, cols), sink_ref[h * grp + grp - 1] * LOG2E, F32)
        for g in range(grp - 2, -1, -1):
            sk = jnp.where(head_lane < (g + 1) * BLOCK, sink_ref[h * grp + g] * LOG2E, sk)
        m = sk
        for t in tiles:
            m = jnp.maximum(m, colmax(t))
        acc = _dot(cv_ref[0, vrows, :], jnp.exp2(tiles[0] - m).astype(BF16))
        if local:
            e_loc = jnp.concatenate([jnp.exp2(t - m) for t in tiles[1:]], axis=0)
            acc = acc + _dot(vt_all[vrows, sb * BLOCK:(sb + 3) * BLOCK], e_loc.astype(BF16))
        den = acc[ones_row:ones_row + 1, :] + jnp.exp2(sk - m)
        return acc / den

    units = [(sb, h) for sb in range(n_sb) for h in range(n_kv)]
    pending = [scores(*un) for un in units[:SCORE_LOOKAHEAD]]
    outs = []
    for u, (sb, h) in enumerate(units):
        tiles = pending.pop(0)
        if u + SCORE_LOOKAHEAD < len(units):
            pending.append(scores(*units[u + SCORE_LOOKAHEAD]))
        outs.append(attend(sb, h, tiles))
        if h == n_kv - 1:
            o_t = jnp.where(kv_row < HEAD_DIM, outs[0], outs[1])
            outs = []
            for g in range(grp):
                o_ref[0, sb * BLOCK:(sb + 1) * BLOCK, g * V7X_LANES:(g + 1) * V7X_LANES] = (
                    o_t[:, g * BLOCK:(g + 1) * BLOCK].T.astype(BF16))


def _attention(sink, q, kd, vt, ckd, cvt, tq, local):
    b, t, aw = q.shape
    ctx_len = cvt.shape[2]
    kvw = cvt.shape[1] // 2
    nsb = tq // BLOCK
    nb = t // BLOCK
    smem = pl.BlockSpec(memory_space=pltpu.SMEM)
    qspec = pl.BlockSpec((1, tq, aw), lambda bi, i: (bi, i, 0))
    in_specs = [smem, qspec]
    args = [sink, q]
    if local:
        prev = lambda bi, i: (bi, jnp.maximum(i * nsb - 1, 0))
        nxt = lambda bi, i: (bi, jnp.minimum((i + 1) * nsb, nb - 1))
        in_specs += [pl.BlockSpec((1, BLOCK, 2 * kvw), lambda bi, i: (*prev(bi, i), 0)),
                     pl.BlockSpec((1, tq, 2 * kvw), lambda bi, i: (bi, i, 0)),
                     pl.BlockSpec((1, BLOCK, 2 * kvw), lambda bi, i: (*nxt(bi, i), 0)),
                     pl.BlockSpec((1, 2 * kvw, BLOCK), lambda bi, i: (bi, 0, prev(bi, i)[1])),
                     pl.BlockSpec((1, 2 * kvw, tq), lambda bi, i: (bi, 0, i)),
                     pl.BlockSpec((1, 2 * kvw, BLOCK), lambda bi, i: (bi, 0, nxt(bi, i)[1]))]
        args += [kd, kd, kd, vt, vt, vt]
    in_specs += [pl.BlockSpec((1, ctx_len, 2 * kvw), lambda bi, i: (bi, 0, 0)),
                 pl.BlockSpec((1, 2 * kvw, ctx_len), lambda bi, i: (bi, 0, 0))]
    args += [ckd, cvt]
    return pl.pallas_call(
        functools.partial(_attn_kernel, local=local),
        grid=(b, t // tq),
        in_specs=in_specs,
        out_specs=qspec,
        out_shape=jax.ShapeDtypeStruct((b, t, aw), BF16),
        compiler_params=_params("arbitrary", "arbitrary"),
        name="attention_local" if local else "attention_ctx",
    )(*args)


def _mxu_const(a):
    return jnp.asarray(a, F32).astype(BF16)


def _dft_tables(n_seq, n2):
    n1 = n_seq // n2
    s2 = np.arange(n2)[:, None, None]
    k1 = np.arange(n1)[None, :, None]
    s1 = np.arange(n1)[None, None, :]
    ang = 2.0 * np.pi * ((k1 * (n2 * s1 + s2)) % n_seq) / n_seq
    m1 = np.concatenate([np.cos(ang), -np.sin(ang)], axis=1)
    a2 = 2.0 * np.pi * ((np.arange(n2)[:, None] * np.arange(n2)[None, :]) % n2) / n2
    c2, s2m = np.cos(a2), np.sin(a2)
    w3 = np.block([[c2, s2m], [-s2m, c2]])
    return _mxu_const(m1), _mxu_const(w3)


def _channel_dft(width):
    d = np.arange(GROUP_DIM)
    a = 2.0 * np.pi * ((d[:, None] * d[None, :]) % GROUP_DIM) / GROUP_DIM
    eye = np.eye(width // GROUP_DIM)
    return np.concatenate([np.kron(eye, np.cos(a)), np.kron(eye, np.sin(a))], axis=0)


def _fft1_kernel(x_ref, m_ref, o_ref):
    o_ref[0] = lax.dot_general(m_ref[...], x_ref[0], (((2,), (1,)), ((0,), (0,))),
                               preferred_element_type=F32).astype(BF16)


def _fft2_kernel(x_ref, w_ref, cs_ref, o_ref, *, scale):
    n2 = o_ref.shape[2]
    for t in range(x_ref.shape[1]):
        z = _dot(w_ref[...], x_ref[0, t])
        zc = jnp.concatenate([z[:n2], z[n2:]], axis=1).astype(BF16)
        o_ref[0, t] = _dot(zc, cs_ref[...]) * scale


def _fourier_latent(f, m1, w3, cs):
    b, s, fw = f.shape
    n2 = w3.shape[0] // 2
    n1 = s // n2
    t2, t1 = 16, 8
    ft = f.reshape(b, n1, n2, fw).transpose(0, 2, 1, 3)
    a = pl.pallas_call(
        _fft1_kernel,
        grid=(b, n2 // t2),
        in_specs=[pl.BlockSpec((1, t2, n1, fw), lambda bi, i: (bi, i, 0, 0)),
                  pl.BlockSpec((t2, 2 * n1, n1), lambda bi, i: (i, 0, 0))],
        out_specs=pl.BlockSpec((1, t2, 2 * n1, fw), lambda bi, i: (bi, i, 0, 0)),
        out_shape=jax.ShapeDtypeStruct((b, n2, 2 * n1, fw), BF16),
        compiler_params=_params("arbitrary", "arbitrary"),
        name="fft_stage1",
    )(ft, m1)
    at = a.reshape(b, n2, 2, n1, fw).transpose(0, 3, 2, 1, 4).reshape(b, n1, 2 * n2, fw)
    y = pl.pallas_call(
        functools.partial(_fft2_kernel, scale=1.0 / math.sqrt(s * GROUP_DIM)),
        grid=(b, n1 // t1),
        in_specs=[pl.BlockSpec((1, t1, 2 * n2, fw), lambda bi, i: (bi, i, 0, 0)),
                  _resident(w3.shape), _resident(cs.shape)],
        out_specs=pl.BlockSpec((1, t1, n2, fw), lambda bi, i: (bi, i, 0, 0)),
        out_shape=jax.ShapeDtypeStruct((b, n1, n2, fw), F32),
        compiler_params=_params("arbitrary", "arbitrary"),
        name="fft_stage2",
    )(at, w3, cs)
    return y.transpose(0, 2, 1, 3).reshape(b, s, fw)


def _fourier_ctx_kernel(f_ref, cs_ref, dl_ref, o_ref, *, scale):
    fw = f_ref.shape[2]
    fc = _dot(f_ref[0], cs_ref[...])
    stacked = jnp.concatenate([fc[:, :fw], fc[:, fw:]], axis=0).astype(BF16)
    o_ref[0] = _dot(dl_ref[...], stacked) * scale


def _fourier_ctx(f, cs_t, dl):
    b, l, fw = f.shape
    return pl.pallas_call(
        functools.partial(_fourier_ctx_kernel, scale=1.0 / math.sqrt(l * GROUP_DIM)),
        grid=(b,),
        in_specs=[pl.BlockSpec((1, l, fw), lambda bi: (bi, 0, 0)),
                  _resident(cs_t.shape), _resident(dl.shape)],
        out_specs=pl.BlockSpec((1, l, fw), lambda bi: (bi, 0, 0)),
        out_shape=jax.ShapeDtypeStruct((b, l, fw), F32),
        compiler_params=_params("arbitrary"),
        name="fourier_ctx",
    )(f, cs_t, dl)


def _out_ff_kernel(x_ref, a_ref, s_ref, f_ref, g1_ref, sc2_ref, sh2_ref, g2_ref,
                   gmix_ref, gpm_ref, gpf_ref, gpo_ref, wo_ref, w1_ref, w2_ref, o_ref, *,
                   ff_chunk, row_chunk):
    aw = a_ref.shape[2]
    sw = s_ref.shape[2]
    gm = gmix_ref[...]
    d_ff = w1_ref.shape[1]
    n_rows = x_ref.shape[1] // row_chunk

    def mix(r):
        rows = slice(r * row_chunk, (r + 1) * row_chunk)
        ycat = jnp.concatenate([
            (_rms(a_ref[0, rows, :].astype(F32)) * gm[:, :aw]).astype(BF16),
            (_rms(s_ref[0, rows, :]) * gm[:, aw:aw + sw]).astype(BF16),
            (_rms(f_ref[0, rows, :]) * gm[:, aw + sw:]).astype(BF16)], axis=1)
        y = _dot(ycat, wo_ref[...])
        x1 = x_ref[0, rows, :] + g1_ref[0] * (_rms(y) * gpm_ref[...])
        return x1, (_rms(x1) * gpf_ref[...] * (1.0 + sc2_ref[0]) + sh2_ref[0]).astype(BF16)

    def mlp_part(h2, c):
        cols = slice(c * ff_chunk, (c + 1) * ff_chunk)
        hid = jnp.maximum(_dot(h2, w1_ref[:, cols]), 0.0)
        return _dot((hid * hid).astype(BF16), w2_ref[cols, :])

    nxt = mix(0)
    for r in range(n_rows):
        x1, h2 = nxt
        ff = mlp_part(h2, 0)
        if r + 1 < n_rows:
            nxt = mix(r + 1)
        for c in range(1, d_ff // ff_chunk):
            ff = ff + mlp_part(h2, c)
        o_ref[0, r * row_chunk:(r + 1) * row_chunk, :] = x1 + g2_ref[0] * (_rms(ff) * gpo_ref[...])


def _out_ff(x, a, sg, fo, g1, sc2, sh2, g2, gmix, gpm, gpf, gpo, wo, w1, w2, tm):
    b, t, d = x.shape
    tok = lambda w: pl.BlockSpec((1, tm, w), lambda bi, i: (bi, i, 0))
    vec = pl.BlockSpec((1, 1, d), lambda bi, i: (bi, 0, 0))
    return pl.pallas_call(
        functools.partial(_out_ff_kernel, ff_chunk=min(w1.shape[1], 1024), row_chunk=min(tm, 4 * BLOCK)),
        grid=(b, t // tm),
        in_specs=[tok(d), tok(a.shape[2]), tok(sg.shape[2]), tok(fo.shape[2]), vec, vec, vec, vec,
                  _resident(gmix.shape), _resident(gpm.shape), _resident(gpf.shape), _resident(gpo.shape),
                  _resident(wo.shape), _resident(w1.shape), _resident(w2.shape)],
        out_specs=tok(d),
        out_shape=jax.ShapeDtypeStruct((b, t, d), F32),
        compiler_params=_params("arbitrary", "arbitrary"),
        name="out_ff",
    )(x, a, sg, fo, g1, sc2, sh2, g2, gmix, gpm, gpf, gpo, wo, w1, w2)


def _rope_tables(seq_len):
    rows = seq_len // GRID_W
    row = jnp.repeat(jnp.arange(rows), GRID_W).astype(F32)
    col = jnp.tile(jnp.arange(GRID_W), rows).astype(F32)
    n_freq = HEAD_DIM // 4
    inv_freq = ROPE_THETA ** (-jnp.arange(n_freq, dtype=F32) / n_freq)
    ang_r = row[:, None] * inv_freq[None, :]
    ang_c = col[:, None] * inv_freq[None, :]
    ang = jnp.concatenate([ang_r, ang_r, ang_c, ang_c], axis=-1)
    reps = V7X_LANES // HEAD_DIM
    cos = jnp.tile(jnp.cos(ang), (1, reps))
    sin = jnp.tile(jnp.sin(ang), (1, reps))
    even = (np.arange(V7X_LANES) // n_freq) % 2 == 0
    return cos, jnp.where(even, -sin, 0.0), jnp.where(even, 0.0, sin)


def kernel(x, c, ctx, c_ctx, w_ada, b_ada, g_pre_mix, w_in, sink, w_sgu, b_sgu, g_sgu,
           g_mix, w_out, g_post_mix, g_pre_ff, w_ff1, w_ff2, g_post_ff):
    b, s, d = x.shape
    ctx_len = ctx.shape[1]
    depth = w_ada.shape[0]
    n_q = sink.shape[1]
    aw = n_q * HEAD_DIM
    n_sgu = w_sgu.shape[1]
    sw = n_sgu * GROUP_DIM
    fw = g_mix.shape[1] - aw - sw
    kvw = (w_in.shape[2] - aw - 2 * sw - fw) // 2
    n_kv = kvw // HEAD_DIM
    grp = n_q // n_kv
    assert n_kv == 2 and 2 * kvw == 2 * V7X_LANES and s % (4 * BLOCK) == 0 and ctx_len % BLOCK == 0
    widths = (aw, kvw, sw, fw)

    rows = -(-(b + 1) // 8) * 8
    cc = jnp.zeros((rows, d), F32).at[:b].set(c).at[b].set(c_ctx)
    mod = _modulation(cc, w_ada, b_ada)

    rope_x = _rope_tables(s)
    rope_c = (jnp.ones((ctx_len, V7X_LANES), F32), jnp.zeros((ctx_len, V7X_LANES), F32),
              jnp.zeros((ctx_len, V7X_LANES), F32))
    ones = _mxu_const(np.kron(np.eye(n_sgu), np.ones((GROUP_DIM, GROUP_DIM))))
    m1, w3 = _dft_tables(s, BLOCK)
    cs_np = _channel_dft(fw)
    cs = _mxu_const(cs_np)
    cs_t = _mxu_const(np.concatenate([cs_np[:fw], cs_np[fw:]], axis=1))
    kl = np.arange(ctx_len)
    al = 2.0 * np.pi * ((kl[:, None] * kl[None, :]) % ctx_len) / ctx_len
    dl = _mxu_const(np.concatenate([np.cos(al), -np.sin(al)], axis=1))

    lane = np.arange(aw)
    perm = np.where(lane % V7X_LANES < HEAD_DIM,
                    (lane // V7X_LANES) * HEAD_DIM + lane % V7X_LANES,
                    (grp + lane // V7X_LANES) * HEAD_DIM + lane % V7X_LANES - HEAD_DIM)

    row2 = lambda v: v.reshape(1, -1)
    h_ctx = ctx
    for l in range(depth):
        last = l == depth - 1
        mx = [m.reshape(b, 1, d) for m in jnp.split(mod[l, :b], 6, axis=-1)]
        mc = [jnp.broadcast_to(m.reshape(1, 1, d), (b, 1, d)) for m in jnp.split(mod[l, b], 6, axis=-1)]
        w_in_b = w_in[l].astype(BF16)
        wsgu = w_sgu[l].reshape(n_sgu * CHUNK, CHUNK).astype(BF16)
        bsgu = jnp.repeat(b_sgu[l].T, GROUP_DIM, axis=1)
        gsgu = g_sgu[l].reshape(1, sw)
        gmix = row2(jnp.concatenate([g_mix[l][:aw][perm], g_mix[l][aw:]]))
        wo = jnp.concatenate([w_out[l][:aw][perm], w_out[l][aw:]], axis=0).astype(BF16)
        shared = (row2(g_pre_mix[l]), w_in_b)
        sgu_w = (wsgu, bsgu, gsgu, ones)
        ff_w = (gmix, row2(g_post_mix[l]), row2(g_pre_ff[l]), row2(g_post_ff[l]), wo,
                w_ff1[l].astype(BF16), w_ff2[l].astype(BF16))

        cq, ckd, cvt, csg, cf = _in_proj(h_ctx, mc[1], mc[0], *shared, rope_c, *sgu_w, ctx_len, widths)
        if not last:
            ca = _attention(sink[l], cq, None, None, ckd, cvt, ctx_len, local=False)
            cfo = _fourier_ctx(cf, cs_t, dl)
            h_ctx = _out_ff(h_ctx, ca, csg, cfo, mc[2], mc[4], mc[3], mc[5], *ff_w, ctx_len)

        q, kd, vt, sg, f = _in_proj(x, mx[1], mx[0], *shared, rope_x, *sgu_w, 4 * BLOCK, widths)
        a = _attention(sink[l], q, kd, vt, ckd, cvt, 8 * BLOCK, local=True)
        fo = _fourier_latent(f, m1, w3, cs)
        x = _out_ff(x, a, sg, fo, mx[2], mx[4], mx[3], mx[5], *ff_w, 8 * BLOCK)
    return x
```

```python
import functools
import math

import numpy as np
import jax
import jax.numpy as jnp
from jax import lax
from jax.experimental import pallas as pl
from jax.experimental.pallas import tpu as pltpu

F32 = jnp.float32
BF16 = jnp.bfloat16

HEAD_DIM = 64
GRID_W = 64
BLOCK = 128
CHUNK = 128
GROUP_DIM = 64
ROPE_THETA = 10000.0
EPS = 1e-6
NEG_INF = -1e30
LOG2E = math.log2(math.e)
FFT_INNER = 128
SCORE_LOOKAHEAD = 2

V7X_LANES = 128
V7X_SUBLANES = 8
V7X_VMEM_BYTES = 64 * 2**20
VMEM_LIMIT = V7X_VMEM_BYTES * 3 // 4


def _params(*sem):
    return pltpu.CompilerParams(dimension_semantics=sem, vmem_limit_bytes=VMEM_LIMIT)


def _resident(shape):
    return pl.BlockSpec(shape, lambda *_: (0,) * len(shape), pipeline_mode=pl.Buffered(1))


def _rms(x):
    return x * lax.rsqrt(jnp.mean(x * x, axis=-1, keepdims=True) + EPS)


def _gelu_tanh(x):
    return x * (0.5 * (1.0 + jnp.tanh(math.sqrt(2.0 / math.pi) * (x + 0.044715 * (x * x * x)))))


def _dot(a, b):
    return jnp.dot(a, b, preferred_element_type=F32)


def _dot_nt(a, b):
    return lax.dot_general(a, b, (((1,), (1,)), ((), ())), preferred_element_type=F32)


def _mod_kernel(cc_ref, w_ref, b_ref, o_ref):
    cc = cc_ref[...]
    s = cc / (1.0 + jnp.exp(-cc))
    o_ref[0] = _dot(s.astype(BF16), w_ref[0].astype(BF16)) + b_ref[0]


def _modulation(cc, w_ada, b_ada):
    depth, d, n = w_ada.shape
    rows = cc.shape[0]
    tn = n // 4
    return pl.pallas_call(
        _mod_kernel,
        grid=(depth, n // tn),
        in_specs=[
            pl.BlockSpec((rows, d), lambda l, j: (0, 0)),
            pl.BlockSpec((1, d, tn), lambda l, j: (l, 0, j)),
            pl.BlockSpec((1, 1, tn), lambda l, j: (l, 0, j)),
        ],
        out_specs=pl.BlockSpec((1, rows, tn), lambda l, j: (l, 0, j)),
        out_shape=jax.ShapeDtypeStruct((depth, rows, n), F32),
        compiler_params=_params("arbitrary", "arbitrary"),
        name="modulation",
    )(cc, w_ada, b_ada.reshape(depth, 1, n))


def _in_proj_kernel(x_ref, sc_ref, sh_ref, g_ref, w_ref, cos_ref, sa_ref, sb_ref,
                    wsgu_ref, bsgu_ref, gsgu_ref, ones_ref,
                    q_ref, kd_ref, v_ref, sgu_ref, f_ref, *, widths, row_chunk, fft_layout):
    aw, kvw, sw, fw = widths
    n_rows = x_ref.shape[1] // row_chunk
    scale = HEAD_DIM ** -0.5 * LOG2E
    n_heads = sw // GROUP_DIM
    lane = lax.broadcasted_iota(jnp.int32, (CHUNK, sw), 1)

    def modulated(r):
        rows = slice(r * row_chunk, (r + 1) * row_chunk)
        hx = _rms(x_ref[0, rows, :]) * g_ref[...] * (1.0 + sc_ref[0]) + sh_ref[0]
        return hx.astype(BF16)

    def split(r, px):
        base = r * row_chunk
        rows = slice(base, base + row_chunk)
        cos, sa, sb = cos_ref[rows, :], sa_ref[rows, :], sb_ref[rows, :]

        def rope(t):
            return t * cos + pltpu.roll(t, V7X_LANES - 16, 1) * sa + pltpu.roll(t, 16, 1) * sb

        for j in range(aw // V7X_LANES):
            qj = rope(px[:, j * V7X_LANES:(j + 1) * V7X_LANES]) * scale
            q_ref[0, rows, j * V7X_LANES:(j + 1) * V7X_LANES] = qj.astype(BF16)

        k = rope(px[:, aw:aw + kvw])
        kr = pltpu.roll(k, HEAD_DIM, 1)
        lo = lax.broadcasted_iota(jnp.int32, k.shape, 1) < HEAD_DIM
        kd_ref[0, rows, :V7X_LANES] = jnp.where(lo, k, kr).astype(BF16)
        kd_ref[0, rows, V7X_LANES:] = jnp.where(lo, kr, k).astype(BF16)
        vt = px[:, aw + kvw:aw + 2 * kvw].T
        head0 = lax.broadcasted_iota(jnp.int32, vt.shape, 0) < HEAD_DIM
        v_ref[0, :kvw, rows] = jnp.where(head0, vt, 1.0).astype(BF16)
        v_ref[0, kvw:, rows] = jnp.where(head0, 1.0, vt).astype(BF16)

        o = aw + 2 * kvw
        u = _gelu_tanh(px[:, o:o + sw])
        gv = _gelu_tanh(px[:, o + sw:o + 2 * sw])
        ss = gv * gv
        hi = ss.astype(BF16)
        lo2 = (ss - hi.astype(F32)).astype(BF16)
        gsum = _dot(hi, ones_ref[...]) + _dot(lo2, ones_ref[...])
        vn = (gv * lax.rsqrt(gsum * (1.0 / GROUP_DIM) + EPS) * gsgu_ref[...]).astype(BF16)
        for c in range(row_chunk // CHUNK):
            crow = slice(c * CHUNK, (c + 1) * CHUNK)
            m = _dot(wsgu_ref[...], vn[crow])
            mixed = m[(n_heads - 1) * CHUNK:]
            for h in range(n_heads - 2, -1, -1):
                mixed = jnp.where(lane < (h + 1) * GROUP_DIM, m[h * CHUNK:(h + 1) * CHUNK], mixed)
            sgu_ref[0, base + c * CHUNK:base + (c + 1) * CHUNK, :] = u[crow] * (mixed + bsgu_ref[...])

        fcols = px[:, o + 2 * sw:]
        if fft_layout:
            for c in range(row_chunk // FFT_INNER):
                f_ref[0, :, base // FFT_INNER + c, :] = fcols[c * FFT_INNER:(c + 1) * FFT_INNER]
        else:
            f_ref[0, rows, :] = fcols.astype(BF16)

    nxt = _dot(modulated(0), w_ref[...])
    for r in range(n_rows):
        px = nxt
        if r + 1 < n_rows:
            nxt = _dot(modulated(r + 1), w_ref[...])
        split(r, px)


def _in_proj(x, sc, sh, g, w_in, rope_tabs, wsgu, bsgu, gsgu, ones, tm, widths, fft_layout):
    b, t, d = x.shape
    aw, kvw, sw, fw = widths
    n_in = w_in.shape[1]
    tok = lambda w: pl.BlockSpec((1, tm, w), lambda bi, i: (bi, i, 0))
    vec = pl.BlockSpec((1, 1, d), lambda bi, i: (bi, 0, 0))
    tab = pl.BlockSpec((tm, V7X_LANES), lambda bi, i: (i, 0))
    if fft_layout:
        f_spec = pl.BlockSpec((1, FFT_INNER, tm // FFT_INNER, fw), lambda bi, i: (bi, 0, i, 0))
        f_shape = jax.ShapeDtypeStruct((b, FFT_INNER, t // FFT_INNER, fw), F32)
    else:
        f_spec, f_shape = tok(fw), jax.ShapeDtypeStruct((b, t, fw), BF16)
    return pl.pallas_call(
        functools.partial(_in_proj_kernel, widths=widths, row_chunk=min(tm, 4 * BLOCK),
                          fft_layout=fft_layout),
        grid=(b, t // tm),
        in_specs=[tok(d), vec, vec, _resident((1, d)), _resident((d, n_in)), tab, tab, tab,
                  _resident(wsgu.shape), _resident(bsgu.shape), _resident(gsgu.shape),
                  _resident(ones.shape)],
        out_specs=[tok(aw), tok(2 * kvw), pl.BlockSpec((1, 2 * kvw, tm), lambda bi, i: (bi, 0, i)),
                   tok(sw), f_spec],
        out_shape=[jax.ShapeDtypeStruct((b, t, aw), BF16),
                   jax.ShapeDtypeStruct((b, t, 2 * kvw), BF16),
                   jax.ShapeDtypeStruct((b, 2 * kvw, t), BF16),
                   jax.ShapeDtypeStruct((b, t, sw), F32),
                   f_shape],
        compiler_params=_params("arbitrary", "arbitrary"),
        name="in_proj",
    )(x, sc, sh, g, w_in, *rope_tabs, wsgu, bsgu, gsgu, ones)


def _attn_kernel(sink_ref, q_ref, *refs, local):
    if local:
        kp_ref, ko_ref, kn_ref, vp_ref, vo_ref, vn_ref, ck_ref, cv_ref, o_ref = refs
    else:
        ck_ref, cv_ref, o_ref = refs
    tq = q_ref.shape[1]
    kvw = ck_ref.shape[2] // 2
    n_kv = kvw // HEAD_DIM
    grp = q_ref.shape[2] // HEAD_DIM // n_kv
    cols = grp * BLOCK
    lane = lax.broadcasted_iota(jnp.int32, (BLOCK, V7X_LANES), 1)
    head_lane = lax.broadcasted_iota(jnp.int32, (1, cols), 1)
    kv_row = lax.broadcasted_iota(jnp.int32, (kvw, cols), 0)
    if local:
        vt_all = jnp.concatenate([vp_ref[0], vo_ref[0], vn_ref[0]], axis=1)
        key = lax.broadcasted_iota(jnp.int32, (BLOCK, cols), 0)
        qpos = lax.broadcasted_iota(jnp.int32, (BLOCK, cols), 1) & (BLOCK - 1)
        ok_prev = key >= qpos
        ok_next = key <= qpos
        not_first = pl.program_id(1) > 0
        not_last = pl.program_id(1) < pl.num_programs(1) - 1
    colmax = lambda t: jnp.max(t, axis=0, keepdims=True)

    n_sb = tq // BLOCK

    def scores(sb, h):
        qrows = slice(sb * BLOCK, (sb + 1) * BLOCK)
        kcols = slice(h * V7X_LANES, (h + 1) * V7X_LANES)
        parts = []
        for g in range(grp):
            head = h * grp + g
            qc = q_ref[0, qrows, (head // 2) * V7X_LANES:(head // 2 + 1) * V7X_LANES]
            keep = (lane < HEAD_DIM) if head % 2 == 0 else (lane >= HEAD_DIM)
            parts.append(jnp.where(keep, qc, jnp.zeros_like(qc)))
        qs = jnp.concatenate(parts, axis=0)
        tiles = [_dot_nt(ck_ref[0, :, kcols], qs)]
        if local:
            k_win = jnp.concatenate(
                [kp_ref[0, :, kcols], ko_ref[0, :, kcols], kn_ref[0, :, kcols]],
                axis=0)[sb * BLOCK:(sb + 3) * BLOCK]
            s_loc = _dot_nt(k_win, qs)
            s_prev = jnp.where(ok_prev, s_loc[:BLOCK], NEG_INF)
            s_next = jnp.where(ok_next, s_loc[2 * BLOCK:], NEG_INF)
            if sb == 0:
                s_prev = jnp.where(not_first, s_prev, NEG_INF)
            if sb == n_sb - 1:
                s_next = jnp.where(not_last, s_next, NEG_INF)
            tiles += [s_prev, s_loc[BLOCK:2 * BLOCK], s_next]
        return tiles

    def attend(sb, h, tiles):
        vrows = slice(h * kvw, (h + 1) * kvw)
        ones_row = ((h + 1) % n_kv) * HEAD_DIM
        sk = jnp.full((1, cols), sink_ref[h * grp + grp - 1] * LOG2E, F32)
        for g in range(grp - 2, -1, -1):
            sk = jnp.where(head_lane < (g + 1) * BLOCK, sink_ref[h * grp + g] * LOG2E, sk)
        m = sk
        for t in tiles:
            m = jnp.maximum(m, colmax(t))
        acc = _dot(cv_ref[0, vrows, :], jnp.exp2(tiles[0] - m).astype(BF16))
        if local:
            e_loc = jnp.concatenate([jnp.exp2(t - m) for t in tiles[1:]], axis=0)
            acc = acc + _dot(vt_all[vrows, sb * BLOCK:(sb + 3) * BLOCK], e_loc.astype(BF16))
        den = acc[ones_row:ones_row + 1, :] + jnp.exp2(sk - m)
        return acc / den

    units = [(sb, h) for sb in range(n_sb) for h in range(n_kv)]
    pending = [scores(*un) for un in units[:SCORE_LOOKAHEAD]]
    outs = []
    for u, (sb, h) in enumerate(units):
        tiles = pending.pop(0)
        if u + SCORE_LOOKAHEAD < len(units):
            pending.append(scores(*units[u + SCORE_LOOKAHEAD]))
        outs.append(attend(sb, h, tiles))
        if h == n_kv - 1:
            o_t = jnp.where(kv_row < HEAD_DIM, outs[0], outs[1])
            outs = []
            for g in range(grp):
                o_ref[0, sb * BLOCK:(sb + 1) * BLOCK, g * V7X_LANES:(g + 1) * V7X_LANES] = (
                    o_t[:, g * BLOCK:(g + 1) * BLOCK].T.astype(BF16))


def _attention(sink, q, kd, vt, ckd, cvt, tq, local):
    b, t, aw = q.shape
    ctx_len = cvt.shape[2]
    kvw = cvt.shape[1] // 2
    nsb = tq // BLOCK
    nb = t // BLOCK
    smem = pl.BlockSpec(memory_space=pltpu.SMEM)
    qspec = pl.BlockSpec((1, tq, aw), lambda bi, i: (bi, i, 0))
    in_specs = [smem, qspec]
    args = [sink, q]
    if local:
        prev = lambda bi, i: (bi, jnp.maximum(i * nsb - 1, 0))
        nxt = lambda bi, i: (bi, jnp.minimum((i + 1) * nsb, nb - 1))
        in_specs += [pl.BlockSpec((1, BLOCK, 2 * kvw), lambda bi, i: (*prev(bi, i), 0)),
                     pl.BlockSpec((1, tq, 2 * kvw), lambda bi, i: (bi, i, 0)),
                     pl.BlockSpec((1, BLOCK, 2 * kvw), lambda bi, i: (*nxt(bi, i), 0)),
                     pl.BlockSpec((1, 2 * kvw, BLOCK), lambda bi, i: (bi, 0, prev(bi, i)[1])),
                     pl.BlockSpec((1, 2 * kvw, tq), lambda bi, i: (bi, 0, i)),
                     pl.BlockSpec((1, 2 * kvw, BLOCK), lambda bi, i: (bi, 0, nxt(bi, i)[1]))]
        args += [kd, kd, kd, vt, vt, vt]
    in_specs += [pl.BlockSpec((1, ctx_len, 2 * kvw), lambda bi, i: (bi, 0, 0)),
                 pl.BlockSpec((1, 2 * kvw, ctx_len), lambda bi, i: (bi, 0, 0))]
    args += [ckd, cvt]
    return pl.pallas_call(
        functools.partial(_attn_kernel, local=local),
        grid=(b, t // tq),
        in_specs=in_specs,
        out_specs=qspec,
        out_shape=jax.ShapeDtypeStruct((b, t, aw), BF16),
        compiler_params=_params("arbitrary", "arbitrary"),
        name="attention_local" if local else "attention_ctx",
    )(*args)


def _mxu_const(a):
    return jnp.asarray(a, F32).astype(BF16)


def _dft_tables(n_seq, n2):
    n1 = n_seq // n2
    s2 = np.arange(n2)[:, None, None]
    k1 = np.arange(n1)[None, :, None]
    s1 = np.arange(n1)[None, None, :]
    ang = 2.0 * np.pi * ((k1 * (n2 * s1 + s2)) % n_seq) / n_seq
    m1 = np.concatenate([np.cos(ang), -np.sin(ang)], axis=1)
    a2 = 2.0 * np.pi * ((np.arange(n2)[:, None] * np.arange(n2)[None, :]) % n2) / n2
    c2, s2m = np.cos(a2), np.sin(a2)
    w3 = np.block([[c2, s2m], [-s2m, c2]])
    return _mxu_const(m1), _mxu_const(w3)


def _channel_dft(width):
    d = np.arange(GROUP_DIM)
    a = 2.0 * np.pi * ((d[:, None] * d[None, :]) % GROUP_DIM) / GROUP_DIM
    eye = np.eye(width // GROUP_DIM)
    return np.concatenate([np.kron(eye, np.cos(a)), np.kron(eye, np.sin(a))], axis=0)


def _fft_kernel(x_ref, m_ref, w_ref, cs_ref, o_ref, a_scr, *, n_stage1, pitch, scale):
    i = pl.program_id(1)
    t2, n1, fw = x_ref.shape[1:]
    n2 = w_ref.shape[0] // 2
    t1 = o_ref.shape[2]
    slabs = fw // V7X_LANES

    @pl.when(i < n_stage1)
    def _stage1():
        for j in range(t2):
            a = _dot(m_ref[j], x_ref[0, j].astype(BF16))
            row0 = pl.multiple_of((i * t2 + j) * pitch, V7X_SUBLANES)
            for sl in range(slabs):
                a_scr[sl, pl.ds(row0, 2 * n1), :] = a[:, sl * V7X_LANES:(sl + 1) * V7X_LANES]

    @pl.when(i >= n_stage1)
    def _stage2():
        for t in range(t1):
            k1 = (i - n_stage1) * t1 + t

            def over_s2(row):
                return jnp.concatenate(
                    [a_scr[sl, pl.ds(row, n2, stride=pitch), :] for sl in range(slabs)], axis=1)

            x = jnp.concatenate([over_s2(k1), over_s2(n1 + k1)], axis=0).astype(BF16)
            z = _dot(w_ref[...], x)
            zc = jnp.concatenate([z[:n2], z[n2:]], axis=1).astype(BF16)
            o_ref[0, :, t, :] = _dot(zc, cs_ref[...]) * scale


def _fourier_latent(ft, m1, w3, cs):
    b, n2, n1, fw = ft.shape
    s = n1 * n2
    t2, t1 = 8, 8
    n_stage1, n_stage2 = n2 // t2, n1 // t1
    tile1 = lambda i: jnp.minimum(i, n_stage1 - 1)
    pitch = 2 * n1 + V7X_SUBLANES
    y = pl.pallas_call(
        functools.partial(_fft_kernel, n_stage1=n_stage1, pitch=pitch,
                          scale=1.0 / math.sqrt(s * GROUP_DIM)),
        grid=(b, n_stage1 + n_stage2),
        in_specs=[pl.BlockSpec((1, t2, n1, fw), lambda bi, i: (bi, tile1(i), 0, 0)),
                  pl.BlockSpec((t2, 2 * n1, n1), lambda bi, i: (tile1(i), 0, 0)),
                  _resident(w3.shape), _resident(cs.shape)],
        out_specs=pl.BlockSpec((1, n2, t1, fw), lambda bi, i: (bi, 0, jnp.maximum(i - n_stage1, 0), 0)),
        out_shape=jax.ShapeDtypeStruct((b, n2, n1, fw), F32),
        scratch_shapes=[pltpu.VMEM((fw // V7X_LANES, n2 * pitch, V7X_LANES), F32)],
        compiler_params=_params("arbitrary", "arbitrary"),
        name="fft",
    )(ft, m1, w3, cs)
    return y.reshape(b, s, fw)


def _fourier_ctx_kernel(f_ref, cs_ref, dl_ref, o_ref, *, scale):
    fw = f_ref.shape[2]
    fc = _dot(f_ref[0], cs_ref[...])
    stacked = jnp.concatenate([fc[:, :fw], fc[:, fw:]], axis=0).astype(BF16)
    o_ref[0] = _dot(dl_ref[...], stacked) * scale


def _fourier_ctx(f, cs_t, dl):
    b, l, fw = f.shape
    return pl.pallas_call(
        functools.partial(_fourier_ctx_kernel, scale=1.0 / math.sqrt(l * GROUP_DIM)),
        grid=(b,),
        in_specs=[pl.BlockSpec((1, l, fw), lambda bi: (bi, 0, 0)),
                  _resident(cs_t.shape), _resident(dl.shape)],
        out_specs=pl.BlockSpec((1, l, fw), lambda bi: (bi, 0, 0)),
        out_shape=jax.ShapeDtypeStruct((b, l, fw), F32),
        compiler_params=_params("arbitrary"),
        name="fourier_ctx",
    )(f, cs_t, dl)


def _out_ff_kernel(x_ref, a_ref, s_ref, f_ref, g1_ref, sc2_ref, sh2_ref, g2_ref,
                   gmix_ref, gpm_ref, gpf_ref, gpo_ref, wo_ref, w1_ref, w2_ref, o_ref, *,
                   ff_chunk, row_chunk):
    aw = a_ref.shape[2]
    sw = s_ref.shape[2]
    gm = gmix_ref[...]
    d_ff = w1_ref.shape[1]
    n_rows = x_ref.shape[1] // row_chunk

    def mix(r):
        rows = slice(r * row_chunk, (r + 1) * row_chunk)
        ycat = jnp.concatenate([
            (_rms(a_ref[0, rows, :].astype(F32)) * gm[:, :aw]).astype(BF16),
            (_rms(s_ref[0, rows, :]) * gm[:, aw:aw + sw]).astype(BF16),
            (_rms(f_ref[0, rows, :]) * gm[:, aw + sw:]).astype(BF16)], axis=1)
        y = _dot(ycat, wo_ref[...])
        x1 = x_ref[0, rows, :] + g1_ref[0] * (_rms(y) * gpm_ref[...])
        return x1, (_rms(x1) * gpf_ref[...] * (1.0 + sc2_ref[0]) + sh2_ref[0]).astype(BF16)

    def mlp_part(h2, c):
        cols = slice(c * ff_chunk, (c + 1) * ff_chunk)
        hid = jnp.maximum(_dot(h2, w1_ref[:, cols]), 0.0)
        return _dot((hid * hid).astype(BF16), w2_ref[cols, :])

    nxt = mix(0)
    for r in range(n_rows):
        x1, h2 = nxt
        ff = mlp_part(h2, 0)
        if r + 1 < n_rows:
            nxt = mix(r + 1)
        for c in range(1, d_ff // ff_chunk):
            ff = ff + mlp_part(h2, c)
        o_ref[0, r * row_chunk:(r + 1) * row_chunk, :] = x1 + g2_ref[0] * (_rms(ff) * gpo_ref[...])


def _out_ff(x, a, sg, fo, g1, sc2, sh2, g2, gmix, gpm, gpf, gpo, wo, w1, w2, tm):
    b, t, d = x.shape
    tok = lambda w: pl.BlockSpec((1, tm, w), lambda bi, i: (bi, i, 0))
    vec = pl.BlockSpec((1, 1, d), lambda bi, i: (bi, 0, 0))
    return pl.pallas_call(
        functools.partial(_out_ff_kernel, ff_chunk=min(w1.shape[1], 1024), row_chunk=min(tm, 4 * BLOCK)),
        grid=(b, t // tm),
        in_specs=[tok(d), tok(a.shape[2]), tok(sg.shape[2]), tok(fo.shape[2]), vec, vec, vec, vec,
                  _resident(gmix.shape), _resident(gpm.shape), _resident(gpf.shape), _resident(gpo.shape),
                  _resident(wo.shape), _resident(w1.shape), _resident(w2.shape)],
        out_specs=tok(d),
        out_shape=jax.ShapeDtypeStruct((b, t, d), F32),
        compiler_params=_params("arbitrary", "arbitrary"),
        name="out_ff",
    )(x, a, sg, fo, g1, sc2, sh2, g2, gmix, gpm, gpf, gpo, wo, w1, w2)


def _rope_tables(seq_len):
    rows = seq_len // GRID_W
    row = jnp.repeat(jnp.arange(rows), GRID_W).astype(F32)
    col = jnp.tile(jnp.arange(GRID_W), rows).astype(F32)
    n_freq = HEAD_DIM // 4
    inv_freq = ROPE_THETA ** (-jnp.arange(n_freq, dtype=F32) / n_freq)
    ang_r = row[:, None] * inv_freq[None, :]
    ang_c = col[:, None] * inv_freq[None, :]
    ang = jnp.concatenate([ang_r, ang_r, ang_c, ang_c], axis=-1)
    reps = V7X_LANES // HEAD_DIM
    cos = jnp.tile(jnp.cos(ang), (1, reps))
    sin = jnp.tile(jnp.sin(ang), (1, reps))
    even = (np.arange(V7X_LANES) // n_freq) % 2 == 0
    return cos, jnp.where(even, -sin, 0.0), jnp.where(even, 0.0, sin)


def kernel(x, c, ctx, c_ctx, w_ada, b_ada, g_pre_mix, w_in, sink, w_sgu, b_sgu, g_sgu,
           g_mix, w_out, g_post_mix, g_pre_ff, w_ff1, w_ff2, g_post_ff):
    b, s, d = x.shape
    ctx_len = ctx.shape[1]
    depth = w_ada.shape[0]
    n_q = sink.shape[1]
    aw = n_q * HEAD_DIM
    n_sgu = w_sgu.shape[1]
    sw = n_sgu * GROUP_DIM
    fw = g_mix.shape[1] - aw - sw
    kvw = (w_in.shape[2] - aw - 2 * sw - fw) // 2
    n_kv = kvw // HEAD_DIM
    grp = n_q // n_kv
    assert n_kv == 2 and kvw == V7X_LANES and s % (8 * BLOCK) == 0 and ctx_len % BLOCK == 0
    assert s % (V7X_SUBLANES * FFT_INNER) == 0 and fw % V7X_LANES == 0
    widths = (aw, kvw, sw, fw)

    rows = -(-(b + 1) // 8) * 8
    cc = jnp.zeros((rows, d), F32).at[:b].set(c).at[b].set(c_ctx)
    mod = _modulation(cc, w_ada, b_ada)

    rope_x = _rope_tables(s)
    rope_c = (jnp.ones((ctx_len, V7X_LANES), F32), jnp.zeros((ctx_len, V7X_LANES), F32),
              jnp.zeros((ctx_len, V7X_LANES), F32))
    ones = _mxu_const(np.kron(np.eye(n_sgu), np.ones((GROUP_DIM, GROUP_DIM))))
    m1, w3 = _dft_tables(s, FFT_INNER)
    cs_np = _channel_dft(fw)
    cs = _mxu_const(cs_np)
    cs_t = _mxu_const(np.concatenate([cs_np[:fw], cs_np[fw:]], axis=1))
    kl = np.arange(ctx_len)
    al = 2.0 * np.pi * ((kl[:, None] * kl[None, :]) % ctx_len) / ctx_len
    dl = _mxu_const(np.concatenate([np.cos(al), -np.sin(al)], axis=1))

    lane = np.arange(aw)
    perm = np.where(lane % V7X_LANES < HEAD_DIM,
                    (lane // V7X_LANES) * HEAD_DIM + lane % V7X_LANES,
                    (grp + lane // V7X_LANES) * HEAD_DIM + lane % V7X_LANES - HEAD_DIM)

    row2 = lambda v: v.reshape(1, -1)
    h_ctx = ctx
    for l in range(depth):
        last = l == depth - 1
        mx = [m.reshape(b, 1, d) for m in jnp.split(mod[l, :b], 6, axis=-1)]
        mc = [jnp.broadcast_to(m.reshape(1, 1, d), (b, 1, d)) for m in jnp.split(mod[l, b], 6, axis=-1)]
        w_in_b = w_in[l].astype(BF16)
        wsgu = w_sgu[l].reshape(n_sgu * CHUNK, CHUNK).astype(BF16)
        bsgu = jnp.repeat(b_sgu[l].T, GROUP_DIM, axis=1)
        gsgu = g_sgu[l].reshape(1, sw)
        gmix = row2(jnp.concatenate([g_mix[l][:aw][perm], g_mix[l][aw:]]))
        wo = jnp.concatenate([w_out[l][:aw][perm], w_out[l][aw:]], axis=0).astype(BF16)
        shared = (row2(g_pre_mix[l]), w_in_b)
        sgu_w = (wsgu, bsgu, gsgu, ones)
        ff_w = (gmix, row2(g_post_mix[l]), row2(g_pre_ff[l]), row2(g_post_ff[l]), wo,
                w_ff1[l].astype(BF16), w_ff2[l].astype(BF16))

        cq, ckd, cvt, csg, cf = _in_proj(h_ctx, mc[1], mc[0], *shared, rope_c, *sgu_w, ctx_len, widths,
                                         fft_layout=False)
        if not last:
            ca = _attention(sink[l], cq, None, None, ckd, cvt, ctx_len, local=False)
            cfo = _fourier_ctx(cf, cs_t, dl)
            h_ctx = _out_ff(h_ctx, ca, csg, cfo, mc[2], mc[4], mc[3], mc[5], *ff_w, ctx_len)

        q, kd, vt, sg, f = _in_proj(x, mx[1], mx[0], *shared, rope_x, *sgu_w, 8 * BLOCK, widths,
                                    fft_layout=True)
        a = _attention(sink[l], q, kd, vt, ckd, cvt, 8 * BLOCK, local=True)
        fo = _fourier_latent(f, m1, w3, cs)
        x = _out_ff(x, a, sg, fo, mx[2], mx[4], mx[3], mx[5], *ff_w, 8 * BLOCK)
    return x
```

```python
import functools
import math

import numpy as np
import jax
import jax.numpy as jnp
from jax import lax
from jax.experimental import pallas as pl
from jax.experimental.pallas import tpu as pltpu

F32 = jnp.float32
BF16 = jnp.bfloat16

HEAD_DIM = 64
GRID_W = 64
BLOCK = 128
CHUNK = 128
GROUP_DIM = 64
ROPE_THETA = 10000.0
EPS = 1e-6
NEG_INF = -1e30
LOG2E = math.log2(math.e)
FFT_INNER = 128
SCORE_LOOKAHEAD = 2

V7X_LANES = 128
V7X_SUBLANES = 8
V7X_VMEM_BYTES = 64 * 2**20
VMEM_LIMIT = V7X_VMEM_BYTES * 3 // 4


def _params(*sem):
    return pltpu.CompilerParams(dimension_semantics=sem, vmem_limit_bytes=VMEM_LIMIT)


def _resident(shape):
    return pl.BlockSpec(shape, lambda *_: (0,) * len(shape), pipeline_mode=pl.Buffered(1))


def _rms(x):
    return x * lax.rsqrt(jnp.mean(x * x, axis=-1, keepdims=True) + EPS)


def _gelu_tanh(x):
    c = math.sqrt(2.0 / math.pi)
    return x * (0.5 + 0.5 * jnp.tanh(x * (c + (c * 0.044715) * (x * x))))


def _dot(a, b):
    return jnp.dot(a, b, preferred_element_type=F32)


def _dot_nt(a, b):
    return lax.dot_general(a, b, (((1,), (1,)), ((), ())), preferred_element_type=F32)


def _mod_kernel(cc_ref, w_ref, b_ref, o_ref):
    cc = cc_ref[...]
    s = cc / (1.0 + jnp.exp(-cc))
    o_ref[0] = _dot(s.astype(BF16), w_ref[0].astype(BF16)) + b_ref[0]


def _modulation(cc, w_ada, b_ada):
    depth, d, n = w_ada.shape
    rows = cc.shape[0]
    tn = n // 4
    return pl.pallas_call(
        _mod_kernel,
        grid=(depth, n // tn),
        in_specs=[
            pl.BlockSpec((rows, d), lambda l, j: (0, 0)),
            pl.BlockSpec((1, d, tn), lambda l, j: (l, 0, j)),
            pl.BlockSpec((1, 1, tn), lambda l, j: (l, 0, j)),
        ],
        out_specs=pl.BlockSpec((1, rows, tn), lambda l, j: (l, 0, j)),
        out_shape=jax.ShapeDtypeStruct((depth, rows, n), F32),
        compiler_params=_params("arbitrary", "arbitrary"),
        name="modulation",
    )(cc, w_ada, b_ada.reshape(depth, 1, n))


def _in_proj_kernel(x_ref, sc_ref, sh_ref, g_ref, w_ref, cos_ref, sa_ref, sb_ref,
                    wsgu_ref, bsgu_ref, gsgu_ref, ones_ref,
                    q_ref, kd_ref, v_ref, sgu_ref, f_ref, *, widths, row_chunk, fft_layout):
    aw, kvw, sw, fw = widths
    n_rows = x_ref.shape[1] // row_chunk
    scale = HEAD_DIM ** -0.5 * LOG2E
    n_heads = sw // GROUP_DIM
    lane = lax.broadcasted_iota(jnp.int32, (CHUNK, sw), 1)

    gain = g_ref[...] * (1.0 + sc_ref[0])

    def modulated(r):
        rows = slice(r * row_chunk, (r + 1) * row_chunk)
        return (_rms(x_ref[0, rows, :]) * gain + sh_ref[0]).astype(BF16)

    def split(r, px):
        base = r * row_chunk
        rows = slice(base, base + row_chunk)
        cos, sa, sb = cos_ref[rows, :], sa_ref[rows, :], sb_ref[rows, :]

        def rope(t):
            return t * cos + pltpu.roll(t, V7X_LANES - 16, 1) * sa + pltpu.roll(t, 16, 1) * sb

        for j in range(aw // V7X_LANES):
            qj = rope(px[:, j * V7X_LANES:(j + 1) * V7X_LANES]) * scale
            q_ref[0, rows, j * V7X_LANES:(j + 1) * V7X_LANES] = qj.astype(BF16)

        k = rope(px[:, aw:aw + kvw])
        kr = pltpu.roll(k, HEAD_DIM, 1)
        lo = lax.broadcasted_iota(jnp.int32, k.shape, 1) < HEAD_DIM
        kd_ref[0, rows, :V7X_LANES] = jnp.where(lo, k, kr).astype(BF16)
        kd_ref[0, rows, V7X_LANES:] = jnp.where(lo, kr, k).astype(BF16)
        vt = px[:, aw + kvw:aw + 2 * kvw].T
        head0 = lax.broadcasted_iota(jnp.int32, vt.shape, 0) < HEAD_DIM
        v_ref[0, :kvw, rows] = jnp.where(head0, vt, 1.0).astype(BF16)
        v_ref[0, kvw:, rows] = jnp.where(head0, 1.0, vt).astype(BF16)

        o = aw + 2 * kvw
        u = _gelu_tanh(px[:, o:o + sw])
        gv = _gelu_tanh(px[:, o + sw:o + 2 * sw])
        ss = gv * gv
        hi = ss.astype(BF16)
        lo2 = (ss - hi.astype(F32)).astype(BF16)
        gsum = _dot(hi, ones_ref[...]) + _dot(lo2, ones_ref[...])
        vn = (gv * lax.rsqrt(gsum * (1.0 / GROUP_DIM) + EPS) * gsgu_ref[...]).astype(BF16)
        for c in range(row_chunk // CHUNK):
            crow = slice(c * CHUNK, (c + 1) * CHUNK)
            m = _dot(wsgu_ref[...], vn[crow])
            mixed = m[(n_heads - 1) * CHUNK:]
            for h in range(n_heads - 2, -1, -1):
                mixed = jnp.where(lane < (h + 1) * GROUP_DIM, m[h * CHUNK:(h + 1) * CHUNK], mixed)
            sgu_ref[0, base + c * CHUNK:base + (c + 1) * CHUNK, :] = u[crow] * (mixed + bsgu_ref[...])

        fcols = px[:, o + 2 * sw:]
        if fft_layout:
            for c in range(row_chunk // FFT_INNER):
                f_ref[0, :, base // FFT_INNER + c, :] = fcols[c * FFT_INNER:(c + 1) * FFT_INNER]
        else:
            f_ref[0, rows, :] = fcols.astype(BF16)

    nxt = _dot(modulated(0), w_ref[...])
    for r in range(n_rows):
        px = nxt
        if r + 1 < n_rows:
            nxt = _dot(modulated(r + 1), w_ref[...])
        split(r, px)


def _in_proj(x, sc, sh, g, w_in, rope_tabs, wsgu, bsgu, gsgu, ones, tm, widths, fft_layout):
    b, t, d = x.shape
    aw, kvw, sw, fw = widths
    n_in = w_in.shape[1]
    tok = lambda w: pl.BlockSpec((1, tm, w), lambda bi, i: (bi, i, 0))
    vec = pl.BlockSpec((1, 1, d), lambda bi, i: (bi, 0, 0))
    tab = pl.BlockSpec((tm, V7X_LANES), lambda bi, i: (i, 0))
    if fft_layout:
        f_spec = pl.BlockSpec((1, FFT_INNER, tm // FFT_INNER, fw), lambda bi, i: (bi, 0, i, 0))
        f_shape = jax.ShapeDtypeStruct((b, FFT_INNER, t // FFT_INNER, fw), F32)
    else:
        f_spec, f_shape = tok(fw), jax.ShapeDtypeStruct((b, t, fw), BF16)
    return pl.pallas_call(
        functools.partial(_in_proj_kernel, widths=widths, row_chunk=min(tm, 4 * BLOCK),
                          fft_layout=fft_layout),
        grid=(b, t // tm),
        in_specs=[tok(d), vec, vec, _resident((1, d)), _resident((d, n_in)), tab, tab, tab,
                  _resident(wsgu.shape), _resident(bsgu.shape), _resident(gsgu.shape),
                  _resident(ones.shape)],
        out_specs=[tok(aw), tok(2 * kvw), pl.BlockSpec((1, 2 * kvw, tm), lambda bi, i: (bi, 0, i)),
                   tok(sw), f_spec],
        out_shape=[jax.ShapeDtypeStruct((b, t, aw), BF16),
                   jax.ShapeDtypeStruct((b, t, 2 * kvw), BF16),
                   jax.ShapeDtypeStruct((b, 2 * kvw, t), BF16),
                   jax.ShapeDtypeStruct((b, t, sw), F32),
                   f_shape],
        compiler_params=_params("arbitrary", "arbitrary"),
        name="in_proj",
    )(x, sc, sh, g, w_in, *rope_tabs, wsgu, bsgu, gsgu, ones)


def _attn_kernel(sink_ref, q_ref, *refs, local):
    if local:
        kp_ref, ko_ref, kn_ref, vp_ref, vo_ref, vn_ref, ck_ref, cv_ref, o_ref = refs
    else:
        ck_ref, cv_ref, o_ref = refs
    tq = q_ref.shape[1]
    kvw = ck_ref.shape[2] // 2
    n_kv = kvw // HEAD_DIM
    grp = q_ref.shape[2] // HEAD_DIM // n_kv
    cols = grp * BLOCK
    lane = lax.broadcasted_iota(jnp.int32, (BLOCK, V7X_LANES), 1)
    head_lane = lax.broadcasted_iota(jnp.int32, (1, cols), 1)
    kv_row = lax.broadcasted_iota(jnp.int32, (kvw, cols), 0)
    if local:
        vt_all = jnp.concatenate([vp_ref[0], vo_ref[0], vn_ref[0]], axis=1)
        key = lax.broadcasted_iota(jnp.int32, (BLOCK, cols), 0)
        qpos = lax.broadcasted_iota(jnp.int32, (BLOCK, cols), 1) & (BLOCK - 1)
        ok_prev = key >= qpos
        ok_next = key <= qpos
        not_first = pl.program_id(1) > 0
        not_last = pl.program_id(1) < pl.num_programs(1) - 1
    colmax = lambda t: jnp.max(t, axis=0, keepdims=True)

    n_sb = tq // BLOCK

    def scores(sb, h):
        qrows = slice(sb * BLOCK, (sb + 1) * BLOCK)
        kcols = slice(h * V7X_LANES, (h + 1) * V7X_LANES)
        parts = []
        for g in range(grp):
            head = h * grp + g
            qc = q_ref[0, qrows, (head // 2) * V7X_LANES:(head // 2 + 1) * V7X_LANES]
            keep = (lane < HEAD_DIM) if head % 2 == 0 else (lane >= HEAD_DIM)
            parts.append(jnp.where(keep, qc, jnp.zeros_like(qc)))
        qs = jnp.concatenate(parts, axis=0)
        tiles = [_dot_nt(ck_ref[0, :, kcols], qs)]
        if local:
            k_win = jnp.concatenate(
                [kp_ref[0, :, kcols], ko_ref[0, :, kcols], kn_ref[0, :, kcols]],
                axis=0)[sb * BLOCK:(sb + 3) * BLOCK]
            s_loc = _dot_nt(k_win, qs)
            s_prev = jnp.where(ok_prev, s_loc[:BLOCK], NEG_INF)
            s_next = jnp.where(ok_next, s_loc[2 * BLOCK:], NEG_INF)
            if sb == 0:
                s_prev = jnp.where(not_first, s_prev, NEG_INF)
            if sb == n_sb - 1:
                s_next = jnp.where(not_last, s_next, NEG_INF)
            tiles += [s_prev, s_loc[BLOCK:2 * BLOCK], s_next]
        return tiles

    def attend(sb, h, tiles):
        vrows = slice(h * kvw, (h + 1) * kvw)
        ones_row = ((h + 1) % n_kv) * HEAD_DIM
        sk = jnp.full((1, cols), sink_ref[h * grp + grp - 1] * LOG2E, F32)
        for g in range(grp - 2, -1, -1):
            sk = jnp.where(head_lane < (g + 1) * BLOCK, sink_ref[h * grp + g] * LOG2E, sk)
        m = sk
        for t in tiles:
            m = jnp.maximum(m, colmax(t))
        acc = _dot(cv_ref[0, vrows, :], jnp.exp2(tiles[0] - m).astype(BF16))
        if local:
            e_loc = jnp.concatenate([jnp.exp2(t - m) for t in tiles[1:]], axis=0)
            acc = acc + _dot(vt_all[vrows, sb * BLOCK:(sb + 3) * BLOCK], e_loc.astype(BF16))
        den = acc[ones_row:ones_row + 1, :] + jnp.exp2(sk - m)
        return acc / den

    units = [(sb, h) for sb in range(n_sb) for h in range(n_kv)]
    pending = [scores(*un) for un in units[:SCORE_LOOKAHEAD]]
    outs = []
    for u, (sb, h) in enumerate(units):
        tiles = pending.pop(0)
        if u + SCORE_LOOKAHEAD < len(units):
            pending.append(scores(*units[u + SCORE_LOOKAHEAD]))
        outs.append(attend(sb, h, tiles))
        if h == n_kv - 1:
            o_t = jnp.where(kv_row < HEAD_DIM, outs[0], outs[1])
            outs = []
            for g in range(grp):
                o_ref[0, sb * BLOCK:(sb + 1) * BLOCK, g * V7X_LANES:(g + 1) * V7X_LANES] = (
                    o_t[:, g * BLOCK:(g + 1) * BLOCK].T.astype(BF16))


def _attention(sink, q, kd, vt, ckd, cvt, tq, local):
    b, t, aw = q.shape
    ctx_len = cvt.shape[2]
    kvw = cvt.shape[1] // 2
    nsb = tq // BLOCK
    nb = t // BLOCK
    smem = pl.BlockSpec(memory_space=pltpu.SMEM)
    qspec = pl.BlockSpec((1, tq, aw), lambda bi, i: (bi, i, 0))
    in_specs = [smem, qspec]
    args = [sink, q]
    if local:
        prev = lambda bi, i: (bi, jnp.maximum(i * nsb - 1, 0))
        nxt = lambda bi, i: (bi, jnp.minimum((i + 1) * nsb, nb - 1))
        in_specs += [pl.BlockSpec((1, BLOCK, 2 * kvw), lambda bi, i: (*prev(bi, i), 0)),
                     pl.BlockSpec((1, tq, 2 * kvw), lambda bi, i: (bi, i, 0)),
                     pl.BlockSpec((1, BLOCK, 2 * kvw), lambda bi, i: (*nxt(bi, i), 0)),
                     pl.BlockSpec((1, 2 * kvw, BLOCK), lambda bi, i: (bi, 0, prev(bi, i)[1])),
                     pl.BlockSpec((1, 2 * kvw, tq), lambda bi, i: (bi, 0, i)),
                     pl.BlockSpec((1, 2 * kvw, BLOCK), lambda bi, i: (bi, 0, nxt(bi, i)[1]))]
        args += [kd, kd, kd, vt, vt, vt]
    in_specs += [pl.BlockSpec((1, ctx_len, 2 * kvw), lambda bi, i: (bi, 0, 0)),
                 pl.BlockSpec((1, 2 * kvw, ctx_len), lambda bi, i: (bi, 0, 0))]
    args += [ckd, cvt]
    return pl.pallas_call(
        functools.partial(_attn_kernel, local=local),
        grid=(b, t // tq),
        in_specs=in_specs,
        out_specs=qspec,
        out_shape=jax.ShapeDtypeStruct((b, t, aw), BF16),
        compiler_params=_params("arbitrary", "arbitrary"),
        name="attention_local" if local else "attention_ctx",
    )(*args)


def _mxu_const(a):
    return jnp.asarray(a, F32).astype(BF16)


def _dft_tables(n_seq, n2):
    n1 = n_seq // n2
    s2 = np.arange(n2)[:, None, None]
    k1 = np.arange(n1)[None, :, None]
    s1 = np.arange(n1)[None, None, :]
    ang = 2.0 * np.pi * ((k1 * (n2 * s1 + s2)) % n_seq) / n_seq
    m1 = np.concatenate([np.cos(ang), -np.sin(ang)], axis=1)
    a2 = 2.0 * np.pi * ((np.arange(n2)[:, None] * np.arange(n2)[None, :]) % n2) / n2
    c2, s2m = np.cos(a2), np.sin(a2)
    w3 = np.block([[c2, s2m], [-s2m, c2]])
    return _mxu_const(m1), _mxu_const(w3)


def _channel_dft(width):
    d = np.arange(GROUP_DIM)
    a = 2.0 * np.pi * ((d[:, None] * d[None, :]) % GROUP_DIM) / GROUP_DIM
    eye = np.eye(width // GROUP_DIM)
    return np.concatenate([np.kron(eye, np.cos(a)), np.kron(eye, np.sin(a))], axis=0)


def _fft_kernel(x_ref, m_ref, w_ref, cs_ref, o_ref, a_scr, *, n_stage1, pitch, scale):
    i = pl.program_id(1)
    t2, n1, fw = x_ref.shape[1:]
    n2 = w_ref.shape[0] // 2
    t1 = o_ref.shape[2]
    slabs = fw // V7X_LANES

    @pl.when(i < n_stage1)
    def _stage1():
        for j in range(t2):
            a = _dot(m_ref[j], x_ref[0, j].astype(BF16))
            row0 = pl.multiple_of((i * t2 + j) * pitch, V7X_SUBLANES)
            for sl in range(slabs):
                a_scr[sl, pl.ds(row0, 2 * n1), :] = a[:, sl * V7X_LANES:(sl + 1) * V7X_LANES]

    @pl.when(i >= n_stage1)
    def _stage2():
        for t in range(t1):
            k1 = (i - n_stage1) * t1 + t

            def over_s2(row):
                return jnp.concatenate(
                    [a_scr[sl, pl.ds(row, n2, stride=pitch), :] for sl in range(slabs)], axis=1)

            x = jnp.concatenate([over_s2(k1), over_s2(n1 + k1)], axis=0).astype(BF16)
            z = _dot(w_ref[...], x)
            zc = jnp.concatenate([z[:n2], z[n2:]], axis=1).astype(BF16)
            o_ref[0, :, t, :] = _dot(zc, cs_ref[...]) * scale


def _fourier_latent(ft, m1, w3, cs):
    b, n2, n1, fw = ft.shape
    s = n1 * n2
    t2, t1 = 32, 32
    n_stage1, n_stage2 = n2 // t2, n1 // t1
    tile1 = lambda i: jnp.minimum(i, n_stage1 - 1)
    pitch = 2 * n1 + V7X_SUBLANES
    y = pl.pallas_call(
        functools.partial(_fft_kernel, n_stage1=n_stage1, pitch=pitch,
                          scale=1.0 / math.sqrt(s * GROUP_DIM)),
        grid=(b, n_stage1 + n_stage2),
        in_specs=[pl.BlockSpec((1, t2, n1, fw), lambda bi, i: (bi, tile1(i), 0, 0)),
                  pl.BlockSpec((t2, 2 * n1, n1), lambda bi, i: (tile1(i), 0, 0)),
                  _resident(w3.shape), _resident(cs.shape)],
        out_specs=pl.BlockSpec((1, n2, t1, fw), lambda bi, i: (bi, 0, jnp.maximum(i - n_stage1, 0), 0)),
        out_shape=jax.ShapeDtypeStruct((b, n2, n1, fw), F32),
        scratch_shapes=[pltpu.VMEM((fw // V7X_LANES, n2 * pitch, V7X_LANES), F32)],
        compiler_params=_params("arbitrary", "arbitrary"),
        name="fft",
    )(ft, m1, w3, cs)
    return y.reshape(b, s, fw)


def _fourier_ctx_kernel(f_ref, cs_ref, dl_ref, o_ref, *, scale):
    fw = f_ref.shape[2]
    fc = _dot(f_ref[0], cs_ref[...])
    stacked = jnp.concatenate([fc[:, :fw], fc[:, fw:]], axis=0).astype(BF16)
    o_ref[0] = _dot(dl_ref[...], stacked) * scale


def _fourier_ctx(f, cs_t, dl):
    b, l, fw = f.shape
    return pl.pallas_call(
        functools.partial(_fourier_ctx_kernel, scale=1.0 / math.sqrt(l * GROUP_DIM)),
        grid=(b,),
        in_specs=[pl.BlockSpec((1, l, fw), lambda bi: (bi, 0, 0)),
                  _resident(cs_t.shape), _resident(dl.shape)],
        out_specs=pl.BlockSpec((1, l, fw), lambda bi: (bi, 0, 0)),
        out_shape=jax.ShapeDtypeStruct((b, l, fw), F32),
        compiler_params=_params("arbitrary"),
        name="fourier_ctx",
    )(f, cs_t, dl)


def _out_ff_kernel(x_ref, a_ref, s_ref, f_ref, g1_ref, sc2_ref, sh2_ref, g2_ref,
                   gmix_ref, gpm_ref, gpf_ref, gpo_ref, wo_ref, w1_ref, w2_ref, o_ref, *,
                   ff_chunk, row_chunk):
    aw = a_ref.shape[2]
    sw = s_ref.shape[2]
    gm = gmix_ref[...]
    d_ff = w1_ref.shape[1]
    n_rows = x_ref.shape[1] // row_chunk
    gate1 = g1_ref[0] * gpm_ref[...]
    gain2 = gpf_ref[...] * (1.0 + sc2_ref[0])
    gate2 = g2_ref[0] * gpo_ref[...]

    def mix(r):
        rows = slice(r * row_chunk, (r + 1) * row_chunk)
        ycat = jnp.concatenate([
            (_rms(a_ref[0, rows, :].astype(F32)) * gm[:, :aw]).astype(BF16),
            (_rms(s_ref[0, rows, :]) * gm[:, aw:aw + sw]).astype(BF16),
            (_rms(f_ref[0, rows, :]) * gm[:, aw + sw:]).astype(BF16)], axis=1)
        y = _dot(ycat, wo_ref[...])
        x1 = x_ref[0, rows, :] + _rms(y) * gate1
        return x1, (_rms(x1) * gain2 + sh2_ref[0]).astype(BF16)

    def mlp_part(h2, c):
        cols = slice(c * ff_chunk, (c + 1) * ff_chunk)
        hid = jnp.maximum(_dot(h2, w1_ref[:, cols]), 0.0)
        return _dot((hid * hid).astype(BF16), w2_ref[cols, :])

    nxt = mix(0)
    for r in range(n_rows):
        x1, h2 = nxt
        ff = mlp_part(h2, 0)
        if r + 1 < n_rows:
            nxt = mix(r + 1)
        for c in range(1, d_ff // ff_chunk):
            ff = ff + mlp_part(h2, c)
        o_ref[0, r * row_chunk:(r + 1) * row_chunk, :] = x1 + _rms(ff) * gate2


def _out_ff(x, a, sg, fo, g1, sc2, sh2, g2, gmix, gpm, gpf, gpo, wo, w1, w2, tm):
    b, t, d = x.shape
    tok = lambda w: pl.BlockSpec((1, tm, w), lambda bi, i: (bi, i, 0))
    vec = pl.BlockSpec((1, 1, d), lambda bi, i: (bi, 0, 0))
    return pl.pallas_call(
        functools.partial(_out_ff_kernel, ff_chunk=min(w1.shape[1], 1024), row_chunk=min(tm, 4 * BLOCK)),
        grid=(b, t // tm),
        in_specs=[tok(d), tok(a.shape[2]), tok(sg.shape[2]), tok(fo.shape[2]), vec, vec, vec, vec,
                  _resident(gmix.shape), _resident(gpm.shape), _resident(gpf.shape), _resident(gpo.shape),
                  _resident(wo.shape), _resident(w1.shape), _resident(w2.shape)],
        out_specs=tok(d),
        out_shape=jax.ShapeDtypeStruct((b, t, d), F32),
        compiler_params=_params("arbitrary", "arbitrary"),
        name="out_ff",
    )(x, a, sg, fo, g1, sc2, sh2, g2, gmix, gpm, gpf, gpo, wo, w1, w2)


def _rope_tables(seq_len):
    rows = seq_len // GRID_W
    row = jnp.repeat(jnp.arange(rows), GRID_W).astype(F32)
    col = jnp.tile(jnp.arange(GRID_W), rows).astype(F32)
    n_freq = HEAD_DIM // 4
    inv_freq = ROPE_THETA ** (-jnp.arange(n_freq, dtype=F32) / n_freq)
    ang_r = row[:, None] * inv_freq[None, :]
    ang_c = col[:, None] * inv_freq[None, :]
    ang = jnp.concatenate([ang_r, ang_r, ang_c, ang_c], axis=-1)
    reps = V7X_LANES // HEAD_DIM
    cos = jnp.tile(jnp.cos(ang), (1, reps))
    sin = jnp.tile(jnp.sin(ang), (1, reps))
    even = (np.arange(V7X_LANES) // n_freq) % 2 == 0
    return cos, jnp.where(even, -sin, 0.0), jnp.where(even, 0.0, sin)


def kernel(x, c, ctx, c_ctx, w_ada, b_ada, g_pre_mix, w_in, sink, w_sgu, b_sgu, g_sgu,
           g_mix, w_out, g_post_mix, g_pre_ff, w_ff1, w_ff2, g_post_ff):
    b, s, d = x.shape
    ctx_len = ctx.shape[1]
    depth = w_ada.shape[0]
    n_q = sink.shape[1]
    aw = n_q * HEAD_DIM
    n_sgu = w_sgu.shape[1]
    sw = n_sgu * GROUP_DIM
    fw = g_mix.shape[1] - aw - sw
    kvw = (w_in.shape[2] - aw - 2 * sw - fw) // 2
    n_kv = kvw // HEAD_DIM
    grp = n_q // n_kv
    assert n_kv == 2 and kvw == V7X_LANES and s % (8 * BLOCK) == 0 and ctx_len % BLOCK == 0
    assert s % (V7X_SUBLANES * FFT_INNER) == 0 and fw % V7X_LANES == 0
    widths = (aw, kvw, sw, fw)

    rows = -(-(b + 1) // 8) * 8
    cc = jnp.zeros((rows, d), F32).at[:b].set(c).at[b].set(c_ctx)
    mod = _modulation(cc, w_ada, b_ada)

    rope_x = _rope_tables(s)
    rope_c = (jnp.ones((ctx_len, V7X_LANES), F32), jnp.zeros((ctx_len, V7X_LANES), F32),
              jnp.zeros((ctx_len, V7X_LANES), F32))
    ones = _mxu_const(np.kron(np.eye(n_sgu), np.ones((GROUP_DIM, GROUP_DIM))))
    m1, w3 = _dft_tables(s, FFT_INNER)
    cs_np = _channel_dft(fw)
    cs = _mxu_const(cs_np)
    cs_t = _mxu_const(np.concatenate([cs_np[:fw], cs_np[fw:]], axis=1))
    kl = np.arange(ctx_len)
    al = 2.0 * np.pi * ((kl[:, None] * kl[None, :]) % ctx_len) / ctx_len
    dl = _mxu_const(np.concatenate([np.cos(al), -np.sin(al)], axis=1))

    lane = np.arange(aw)
    perm = np.where(lane % V7X_LANES < HEAD_DIM,
                    (lane // V7X_LANES) * HEAD_DIM + lane % V7X_LANES,
                    (grp + lane // V7X_LANES) * HEAD_DIM + lane % V7X_LANES - HEAD_DIM)

    row2 = lambda v: v.reshape(1, -1)
    h_ctx = ctx
    for l in range(depth):
        last = l == depth - 1
        mx = [m.reshape(b, 1, d) for m in jnp.split(mod[l, :b], 6, axis=-1)]
        mc = [jnp.broadcast_to(m.reshape(1, 1, d), (b, 1, d)) for m in jnp.split(mod[l, b], 6, axis=-1)]
        w_in_b = w_in[l].astype(BF16)
        wsgu = w_sgu[l].reshape(n_sgu * CHUNK, CHUNK).astype(BF16)
        bsgu = jnp.repeat(b_sgu[l].T, GROUP_DIM, axis=1)
        gsgu = g_sgu[l].reshape(1, sw)
        gmix = row2(jnp.concatenate([g_mix[l][:aw][perm], g_mix[l][aw:]]))
        wo = jnp.concatenate([w_out[l][:aw][perm], w_out[l][aw:]], axis=0).astype(BF16)
        shared = (row2(g_pre_mix[l]), w_in_b)
        sgu_w = (wsgu, bsgu, gsgu, ones)
        ff_w = (gmix, row2(g_post_mix[l]), row2(g_pre_ff[l]), row2(g_post_ff[l]), wo,
                w_ff1[l].astype(BF16), w_ff2[l].astype(BF16))

        cq, ckd, cvt, csg, cf = _in_proj(h_ctx, mc[1], mc[0], *shared, rope_c, *sgu_w, ctx_len, widths,
                                         fft_layout=False)
        if not last:
            ca = _attention(sink[l], cq, None, None, ckd, cvt, ctx_len, local=False)
            cfo = _fourier_ctx(cf, cs_t, dl)
            h_ctx = _out_ff(h_ctx, ca, csg, cfo, mc[2], mc[4], mc[3], mc[5], *ff_w, ctx_len)

        q, kd, vt, sg, f = _in_proj(x, mx[1], mx[0], *shared, rope_x, *sgu_w, 8 * BLOCK, widths,
                                    fft_layout=True)
        a = _attention(sink[l], q, kd, vt, ckd, cvt, 8 * BLOCK, local=True)
        fo = _fourier_latent(f, m1, w3, cs)
        x = _out_ff(x, a, sg, fo, mx[2], mx[4], mx[3], mx[5], *ff_w, 8 * BLOCK)
    return x
```

```python
import functools
import math

import numpy as np
import jax
import jax.numpy as jnp
from jax import lax
from jax.experimental import pallas as pl
from jax.experimental.pallas import tpu as pltpu

F32 = jnp.float32
BF16 = jnp.bfloat16

HEAD_DIM = 64
GRID_W = 64
BLOCK = 128
CHUNK = 128
GROUP_DIM = 64
ROPE_THETA = 10000.0
EPS = 1e-6
NEG_INF = -1e30
LOG2E = math.log2(math.e)
MOD_SHIFT1, MOD_SCALE1, MOD_GATE1, MOD_SHIFT2, MOD_SCALE2, MOD_GATE2 = range(6)
FFT_INNER = 128
SCORE_LOOKAHEAD = 2

V7X_LANES = 128
V7X_SUBLANES = 8
V7X_VMEM_BYTES = 64 * 2**20
VMEM_LIMIT = V7X_VMEM_BYTES * 3 // 4


def _params(*sem):
    return pltpu.CompilerParams(dimension_semantics=sem, vmem_limit_bytes=VMEM_LIMIT)


def _resident(shape):
    return pl.BlockSpec(shape, lambda *_: (0,) * len(shape), pipeline_mode=pl.Buffered(1))


def _rms(x):
    return x * lax.rsqrt(jnp.mean(x * x, axis=-1, keepdims=True) + EPS)


def _gelu_tanh(x):
    c = math.sqrt(2.0 / math.pi)
    return x * (0.5 + 0.5 * jnp.tanh(x * (c + (c * 0.044715) * (x * x))))


def _dot(a, b):
    return jnp.dot(a, b, preferred_element_type=F32)


def _dot_nt(a, b):
    return lax.dot_general(a, b, (((1,), (1,)), ((), ())), preferred_element_type=F32)


def _mod_kernel(cc_ref, w_ref, b_ref, o_ref):
    cc = cc_ref[...]
    s = cc / (1.0 + jnp.exp(-cc))
    o_ref[0] = _dot(s.astype(BF16), w_ref[0].astype(BF16)) + b_ref[0]


def _modulation(cc, w_ada, b_ada):
    depth, d, n = w_ada.shape
    rows = cc.shape[0]
    tn = n // 4
    return pl.pallas_call(
        _mod_kernel,
        grid=(depth, n // tn),
        in_specs=[
            pl.BlockSpec((rows, d), lambda l, j: (0, 0)),
            pl.BlockSpec((1, d, tn), lambda l, j: (l, 0, j)),
            pl.BlockSpec((1, 1, tn), lambda l, j: (l, 0, j)),
        ],
        out_specs=pl.BlockSpec((1, rows, tn), lambda l, j: (l, 0, j)),
        out_shape=jax.ShapeDtypeStruct((depth, rows, n), F32),
        compiler_params=_params("arbitrary", "arbitrary"),
        name="modulation",
    )(cc, w_ada, b_ada.reshape(depth, 1, n))


def _in_proj_kernel(x_ref, sc_ref, sh_ref, g_ref, w_ref, cos_ref, sa_ref, sb_ref,
                    wsgu_ref, bsgu_ref, gsgu_ref, ones_ref,
                    q_ref, kd_ref, v_ref, sgu_ref, f_ref, *, widths, row_chunk, fft_layout):
    aw, kvw, sw, fw = widths
    n_rows = x_ref.shape[1] // row_chunk
    scale = HEAD_DIM ** -0.5 * LOG2E
    n_heads = sw // GROUP_DIM
    lane = lax.broadcasted_iota(jnp.int32, (CHUNK, sw), 1)

    gain = g_ref[...] * (1.0 + sc_ref[0, 0])
    shift = sh_ref[0, 0]

    def modulated(r):
        rows = slice(r * row_chunk, (r + 1) * row_chunk)
        return (_rms(x_ref[0, rows, :]) * gain + shift).astype(BF16)

    def split(r, px):
        base = r * row_chunk
        rows = slice(base, base + row_chunk)
        cos, sa, sb = cos_ref[rows, :], sa_ref[rows, :], sb_ref[rows, :]

        def rope(t):
            return t * cos + pltpu.roll(t, V7X_LANES - 16, 1) * sa + pltpu.roll(t, 16, 1) * sb

        for j in range(aw // V7X_LANES):
            qj = rope(px[:, j * V7X_LANES:(j + 1) * V7X_LANES]) * scale
            q_ref[0, rows, j * V7X_LANES:(j + 1) * V7X_LANES] = qj.astype(BF16)

        k = rope(px[:, aw:aw + kvw])
        kr = pltpu.roll(k, HEAD_DIM, 1)
        lo = lax.broadcasted_iota(jnp.int32, k.shape, 1) < HEAD_DIM
        kd_ref[0, rows, :V7X_LANES] = jnp.where(lo, k, kr).astype(BF16)
        kd_ref[0, rows, V7X_LANES:] = jnp.where(lo, kr, k).astype(BF16)
        vt = px[:, aw + kvw:aw + 2 * kvw].T
        head0 = lax.broadcasted_iota(jnp.int32, vt.shape, 0) < HEAD_DIM
        v_ref[0, :kvw, rows] = jnp.where(head0, vt, 1.0).astype(BF16)
        v_ref[0, kvw:, rows] = jnp.where(head0, 1.0, vt).astype(BF16)

        o = aw + 2 * kvw
        u = _gelu_tanh(px[:, o:o + sw])
        gv = _gelu_tanh(px[:, o + sw:o + 2 * sw])
        ss = gv * gv
        hi = ss.astype(BF16)
        lo2 = (ss - hi.astype(F32)).astype(BF16)
        gsum = _dot(hi, ones_ref[...]) + _dot(lo2, ones_ref[...])
        vn = (gv * lax.rsqrt(gsum * (1.0 / GROUP_DIM) + EPS) * gsgu_ref[...]).astype(BF16)
        for c in range(row_chunk // CHUNK):
            crow = slice(c * CHUNK, (c + 1) * CHUNK)
            m = _dot(wsgu_ref[...], vn[crow])
            mixed = m[(n_heads - 1) * CHUNK:]
            for h in range(n_heads - 2, -1, -1):
                mixed = jnp.where(lane < (h + 1) * GROUP_DIM, m[h * CHUNK:(h + 1) * CHUNK], mixed)
            sgu_ref[0, base + c * CHUNK:base + (c + 1) * CHUNK, :] = u[crow] * (mixed + bsgu_ref[...])

        fcols = px[:, o + 2 * sw:]
        if fft_layout:
            for c in range(row_chunk // FFT_INNER):
                f_ref[0, :, base // FFT_INNER + c, :] = fcols[c * FFT_INNER:(c + 1) * FFT_INNER]
        else:
            f_ref[0, rows, :] = fcols.astype(BF16)

    nxt = _dot(modulated(0), w_ref[...])
    for r in range(n_rows):
        px = nxt
        if r + 1 < n_rows:
            nxt = _dot(modulated(r + 1), w_ref[...])
        split(r, px)


def _mod_spec(d, layer, row, part):
    return pl.BlockSpec((1, 1, 1, d), lambda bi, i: (layer, bi if row is None else row, 0, part))


def _in_proj(x, mod, layer, mod_row, g, w_in, rope_tabs, wsgu, bsgu, gsgu, ones, tm, widths, fft_layout):
    b, t, d = x.shape
    aw, kvw, sw, fw = widths
    n_in = w_in.shape[1]
    tok = lambda w: pl.BlockSpec((1, tm, w), lambda bi, i: (bi, i, 0))
    vec = lambda part: _mod_spec(d, layer, mod_row, part)
    tab = pl.BlockSpec((tm, V7X_LANES), lambda bi, i: (i, 0))
    if fft_layout:
        f_spec = pl.BlockSpec((1, FFT_INNER, tm // FFT_INNER, fw), lambda bi, i: (bi, 0, i, 0))
        f_shape = jax.ShapeDtypeStruct((b, FFT_INNER, t // FFT_INNER, fw), F32)
    else:
        f_spec, f_shape = tok(fw), jax.ShapeDtypeStruct((b, t, fw), BF16)
    return pl.pallas_call(
        functools.partial(_in_proj_kernel, widths=widths, row_chunk=min(tm, 4 * BLOCK),
                          fft_layout=fft_layout),
        grid=(b, t // tm),
        in_specs=[tok(d), vec(MOD_SCALE1), vec(MOD_SHIFT1), _resident((1, d)), _resident((d, n_in)),
                  tab, tab, tab,
                  _resident(wsgu.shape), _resident(bsgu.shape), _resident(gsgu.shape),
                  _resident(ones.shape)],
        out_specs=[tok(aw), tok(2 * kvw), pl.BlockSpec((1, 2 * kvw, tm), lambda bi, i: (bi, 0, i)),
                   tok(sw), f_spec],
        out_shape=[jax.ShapeDtypeStruct((b, t, aw), BF16),
                   jax.ShapeDtypeStruct((b, t, 2 * kvw), BF16),
                   jax.ShapeDtypeStruct((b, 2 * kvw, t), BF16),
                   jax.ShapeDtypeStruct((b, t, sw), F32),
                   f_shape],
        compiler_params=_params("arbitrary", "arbitrary"),
        name="in_proj",
    )(x, mod, mod, g, w_in, *rope_tabs, wsgu, bsgu, gsgu, ones)


def _attn_kernel(sink_ref, q_ref, *refs, local):
    if local:
        kp_ref, ko_ref, kn_ref, vp_ref, vo_ref, vn_ref, ck_ref, cv_ref, o_ref = refs
    else:
        ck_ref, cv_ref, o_ref = refs
    tq = q_ref.shape[1]
    kvw = ck_ref.shape[2] // 2
    n_kv = kvw // HEAD_DIM
    grp = q_ref.shape[2] // HEAD_DIM // n_kv
    cols = grp * BLOCK
    lane = lax.broadcasted_iota(jnp.int32, (BLOCK, V7X_LANES), 1)
    head_lane = lax.broadcasted_iota(jnp.int32, (1, cols), 1)
    kv_row = lax.broadcasted_iota(jnp.int32, (kvw, cols), 0)
    if local:
        vt_all = jnp.concatenate([vp_ref[0], vo_ref[0], vn_ref[0]], axis=1)
        key = lax.broadcasted_iota(jnp.int32, (BLOCK, cols), 0)
        qpos = lax.broadcasted_iota(jnp.int32, (BLOCK, cols), 1) & (BLOCK - 1)
        ok_prev = key >= qpos
        ok_next = key <= qpos
        not_first = pl.program_id(1) > 0
        not_last = pl.program_id(1) < pl.num_programs(1) - 1
    colmax = lambda t: jnp.max(t, axis=0, keepdims=True)

    n_sb = tq // BLOCK

    def scores(sb, h):
        qrows = slice(sb * BLOCK, (sb + 1) * BLOCK)
        kcols = slice(h * V7X_LANES, (h + 1) * V7X_LANES)
        parts = []
        for g in range(grp):
            head = h * grp + g
            qc = q_ref[0, qrows, (head // 2) * V7X_LANES:(head // 2 + 1) * V7X_LANES]
            keep = (lane < HEAD_DIM) if head % 2 == 0 else (lane >= HEAD_DIM)
            parts.append(jnp.where(keep, qc, jnp.zeros_like(qc)))
        qs = jnp.concatenate(parts, axis=0)
        tiles = [_dot_nt(ck_ref[0, :, kcols], qs)]
        if local:
            k_win = jnp.concatenate(
                [kp_ref[0, :, kcols], ko_ref[0, :, kcols], kn_ref[0, :, kcols]],
                axis=0)[sb * BLOCK:(sb + 3) * BLOCK]
            s_loc = _dot_nt(k_win, qs)
            s_prev = jnp.where(ok_prev, s_loc[:BLOCK], NEG_INF)
            s_next = jnp.where(ok_next, s_loc[2 * BLOCK:], NEG_INF)
            if sb == 0:
                s_prev = jnp.where(not_first, s_prev, NEG_INF)
            if sb == n_sb - 1:
                s_next = jnp.where(not_last, s_next, NEG_INF)
            tiles += [s_prev, s_loc[BLOCK:2 * BLOCK], s_next]
        return tiles

    def attend(sb, h, tiles):
        vrows = slice(h * kvw, (h + 1) * kvw)
        ones_row = ((h + 1) % n_kv) * HEAD_DIM
        sk = jnp.full((1, cols), sink_ref[h * grp + grp - 1] * LOG2E, F32)
        for g in range(grp - 2, -1, -1):
            sk = jnp.where(head_lane < (g + 1) * BLOCK, sink_ref[h * grp + g] * LOG2E, sk)
        m = sk
        for t in tiles:
            m = jnp.maximum(m, colmax(t))
        acc = _dot(cv_ref[0, vrows, :], jnp.exp2(tiles[0] - m).astype(BF16))
        if local:
            e_loc = jnp.concatenate([jnp.exp2(t - m) for t in tiles[1:]], axis=0)
            acc = acc + _dot(vt_all[vrows, sb * BLOCK:(sb + 3) * BLOCK], e_loc.astype(BF16))
        den = acc[ones_row:ones_row + 1, :] + jnp.exp2(sk - m)
        return acc / den

    units = [(sb, h) for sb in range(n_sb) for h in range(n_kv)]
    pending = [scores(*un) for un in units[:SCORE_LOOKAHEAD]]
    outs = []
    for u, (sb, h) in enumerate(units):
        tiles = pending.pop(0)
        if u + SCORE_LOOKAHEAD < len(units):
            pending.append(scores(*units[u + SCORE_LOOKAHEAD]))
        outs.append(attend(sb, h, tiles))
        if h == n_kv - 1:
            o_t = jnp.where(kv_row < HEAD_DIM, outs[0], outs[1])
            outs = []
            for g in range(grp):
                o_ref[0, sb * BLOCK:(sb + 1) * BLOCK, g * V7X_LANES:(g + 1) * V7X_LANES] = (
                    o_t[:, g * BLOCK:(g + 1) * BLOCK].T.astype(BF16))


def _attention(sink, q, kd, vt, ckd, cvt, tq, local):
    b, t, aw = q.shape
    ctx_len = cvt.shape[2]
    kvw = cvt.shape[1] // 2
    nsb = tq // BLOCK
    nb = t // BLOCK
    smem = pl.BlockSpec(memory_space=pltpu.SMEM)
    qspec = pl.BlockSpec((1, tq, aw), lambda bi, i: (bi, i, 0))
    in_specs = [smem, qspec]
    args = [sink, q]
    if local:
        prev = lambda bi, i: (bi, jnp.maximum(i * nsb - 1, 0))
        nxt = lambda bi, i: (bi, jnp.minimum((i + 1) * nsb, nb - 1))
        in_specs += [pl.BlockSpec((1, BLOCK, 2 * kvw), lambda bi, i: (*prev(bi, i), 0)),
                     pl.BlockSpec((1, tq, 2 * kvw), lambda bi, i: (bi, i, 0)),
                     pl.BlockSpec((1, BLOCK, 2 * kvw), lambda bi, i: (*nxt(bi, i), 0)),
                     pl.BlockSpec((1, 2 * kvw, BLOCK), lambda bi, i: (bi, 0, prev(bi, i)[1])),
                     pl.BlockSpec((1, 2 * kvw, tq), lambda bi, i: (bi, 0, i)),
                     pl.BlockSpec((1, 2 * kvw, BLOCK), lambda bi, i: (bi, 0, nxt(bi, i)[1]))]
        args += [kd, kd, kd, vt, vt, vt]
    in_specs += [pl.BlockSpec((1, ctx_len, 2 * kvw), lambda bi, i: (bi, 0, 0)),
                 pl.BlockSpec((1, 2 * kvw, ctx_len), lambda bi, i: (bi, 0, 0))]
    args += [ckd, cvt]
    return pl.pallas_call(
        functools.partial(_attn_kernel, local=local),
        grid=(b, t // tq),
        in_specs=in_specs,
        out_specs=qspec,
        out_shape=jax.ShapeDtypeStruct((b, t, aw), BF16),
        compiler_params=_params("arbitrary", "arbitrary"),
        name="attention_local" if local else "attention_ctx",
    )(*args)


def _mxu_const(a):
    return jnp.asarray(a, F32).astype(BF16)


def _dft_tables(n_seq, n2):
    n1 = n_seq // n2
    s2 = np.arange(n2)[:, None, None]
    k1 = np.arange(n1)[None, :, None]
    s1 = np.arange(n1)[None, None, :]
    ang = 2.0 * np.pi * ((k1 * (n2 * s1 + s2)) % n_seq) / n_seq
    m1 = np.concatenate([np.cos(ang), -np.sin(ang)], axis=1)
    a2 = 2.0 * np.pi * ((np.arange(n2)[:, None] * np.arange(n2)[None, :]) % n2) / n2
    c2, s2m = np.cos(a2), np.sin(a2)
    w3 = np.block([[c2, s2m], [-s2m, c2]])
    return _mxu_const(m1), _mxu_const(w3)


def _channel_dft(width):
    d = np.arange(GROUP_DIM)
    a = 2.0 * np.pi * ((d[:, None] * d[None, :]) % GROUP_DIM) / GROUP_DIM
    eye = np.eye(width // GROUP_DIM)
    return np.concatenate([np.kron(eye, np.cos(a)), np.kron(eye, np.sin(a))], axis=0)


def _fft_kernel(x_ref, m_ref, w_ref, cs_ref, o_ref, a_scr, *, n_stage1, pitch, scale):
    i = pl.program_id(1)
    t2, n1, fw = x_ref.shape[1:]
    n2 = w_ref.shape[0] // 2
    t1 = o_ref.shape[2]
    slabs = fw // V7X_LANES

    @pl.when(i < n_stage1)
    def _stage1():
        for j in range(t2):
            a = _dot(m_ref[j], x_ref[0, j].astype(BF16))
            row0 = pl.multiple_of((i * t2 + j) * pitch, V7X_SUBLANES)
            for sl in range(slabs):
                a_scr[sl, pl.ds(row0, 2 * n1), :] = a[:, sl * V7X_LANES:(sl + 1) * V7X_LANES]

    @pl.when(i >= n_stage1)
    def _stage2():
        for t in range(t1):
            k1 = (i - n_stage1) * t1 + t

            def over_s2(row):
                return jnp.concatenate(
                    [a_scr[sl, pl.ds(row, n2, stride=pitch), :] for sl in range(slabs)], axis=1)

            x = jnp.concatenate([over_s2(k1), over_s2(n1 + k1)], axis=0).astype(BF16)
            z = _dot(w_ref[...], x)
            zc = jnp.concatenate([z[:n2], z[n2:]], axis=1).astype(BF16)
            o_ref[0, :, t, :] = _dot(zc, cs_ref[...]) * scale


def _fourier_latent(ft, m1, w3, cs):
    b, n2, n1, fw = ft.shape
    s = n1 * n2
    t2, t1 = 32, 32
    n_stage1, n_stage2 = n2 // t2, n1 // t1
    tile1 = lambda i: jnp.minimum(i, n_stage1 - 1)
    pitch = 2 * n1 + V7X_SUBLANES
    y = pl.pallas_call(
        functools.partial(_fft_kernel, n_stage1=n_stage1, pitch=pitch,
                          scale=1.0 / math.sqrt(s * GROUP_DIM)),
        grid=(b, n_stage1 + n_stage2),
        in_specs=[pl.BlockSpec((1, t2, n1, fw), lambda bi, i: (bi, tile1(i), 0, 0)),
                  pl.BlockSpec((t2, 2 * n1, n1), lambda bi, i: (tile1(i), 0, 0)),
                  _resident(w3.shape), _resident(cs.shape)],
        out_specs=pl.BlockSpec((1, n2, t1, fw), lambda bi, i: (bi, 0, jnp.maximum(i - n_stage1, 0), 0)),
        out_shape=jax.ShapeDtypeStruct((b, n2, n1, fw), F32),
        scratch_shapes=[pltpu.VMEM((fw // V7X_LANES, n2 * pitch, V7X_LANES), F32)],
        compiler_params=_params("arbitrary", "arbitrary"),
        name="fft",
    )(ft, m1, w3, cs)
    return y.reshape(b, s, fw)


def _fourier_ctx_kernel(f_ref, cs_ref, dl_ref, o_ref, *, scale):
    fw = f_ref.shape[2]
    fc = _dot(f_ref[0], cs_ref[...])
    stacked = jnp.concatenate([fc[:, :fw], fc[:, fw:]], axis=0).astype(BF16)
    o_ref[0] = _dot(dl_ref[...], stacked) * scale


def _fourier_ctx(f, cs_t, dl):
    b, l, fw = f.shape
    return pl.pallas_call(
        functools.partial(_fourier_ctx_kernel, scale=1.0 / math.sqrt(l * GROUP_DIM)),
        grid=(b,),
        in_specs=[pl.BlockSpec((1, l, fw), lambda bi: (bi, 0, 0)),
                  _resident(cs_t.shape), _resident(dl.shape)],
        out_specs=pl.BlockSpec((1, l, fw), lambda bi: (bi, 0, 0)),
        out_shape=jax.ShapeDtypeStruct((b, l, fw), F32),
        compiler_params=_params("arbitrary"),
        name="fourier_ctx",
    )(f, cs_t, dl)


def _out_ff_kernel(x_ref, a_ref, s_ref, f_ref, g1_ref, sc2_ref, sh2_ref, g2_ref,
                   gmix_ref, gpm_ref, gpf_ref, gpo_ref, wo_ref, w1_ref, w2_ref, o_ref, *,
                   ff_chunk, row_chunks):
    aw = a_ref.shape[2]
    sw = s_ref.shape[2]
    gm = gmix_ref[...]
    d_ff = w1_ref.shape[1]
    n_rows = len(row_chunks)
    starts = [sum(row_chunks[:r]) for r in range(n_rows + 1)]
    gate1 = g1_ref[0, 0] * gpm_ref[...]
    gain2 = gpf_ref[...] * (1.0 + sc2_ref[0, 0])
    shift2 = sh2_ref[0, 0]
    gate2 = g2_ref[0, 0] * gpo_ref[...]

    def mix(r):
        rows = slice(starts[r], starts[r + 1])
        ycat = jnp.concatenate([
            (_rms(a_ref[0, rows, :].astype(F32)) * gm[:, :aw]).astype(BF16),
            (_rms(s_ref[0, rows, :]) * gm[:, aw:aw + sw]).astype(BF16),
            (_rms(f_ref[0, rows, :]) * gm[:, aw + sw:]).astype(BF16)], axis=1)
        y = _dot(ycat, wo_ref[...])
        x1 = x_ref[0, rows, :] + _rms(y) * gate1
        return x1, (_rms(x1) * gain2 + shift2).astype(BF16)

    n_ff = d_ff // ff_chunk

    def up(h2, c):
        hid = jnp.maximum(_dot(h2, w1_ref[:, c * ff_chunk:(c + 1) * ff_chunk]), 0.0)
        return (hid * hid).astype(BF16)

    def down(hid, c):
        return _dot(hid, w2_ref[c * ff_chunk:(c + 1) * ff_chunk, :])

    nxt = mix(0)
    for r in range(n_rows):
        x1, h2 = nxt
        hid = up(h2, 0)
        ff = None
        for c in range(n_ff):
            hid_next = up(h2, c + 1) if c + 1 < n_ff else None
            if c == 0 and r + 1 < n_rows:
                nxt = mix(r + 1)
            part = down(hid, c)
            ff = part if ff is None else ff + part
            hid = hid_next
        o_ref[0, starts[r]:starts[r + 1], :] = x1 + _rms(ff) * gate2


def _out_ff(x, a, sg, fo, mod, layer, mod_row, gmix, gpm, gpf, gpo, wo, w1, w2, tm):
    b, t, d = x.shape
    tok = lambda w: pl.BlockSpec((1, tm, w), lambda bi, i: (bi, i, 0))
    vec = lambda part: _mod_spec(d, layer, mod_row, part)
    row_chunks = (min(tm, 4 * BLOCK),) * (tm // min(tm, 4 * BLOCK))
    return pl.pallas_call(
        functools.partial(_out_ff_kernel, ff_chunk=min(w1.shape[1], 1024), row_chunks=row_chunks),
        grid=(b, t // tm),
        in_specs=[tok(d), tok(a.shape[2]), tok(sg.shape[2]), tok(fo.shape[2]),
                  vec(MOD_GATE1), vec(MOD_SCALE2), vec(MOD_SHIFT2), vec(MOD_GATE2),
                  _resident(gmix.shape), _resident(gpm.shape), _resident(gpf.shape), _resident(gpo.shape),
                  _resident(wo.shape), _resident(w1.shape), _resident(w2.shape)],
        out_specs=tok(d),
        out_shape=jax.ShapeDtypeStruct((b, t, d), F32),
        compiler_params=_params("arbitrary", "arbitrary"),
        name="out_ff",
    )(x, a, sg, fo, mod, mod, mod, mod, gmix, gpm, gpf, gpo, wo, w1, w2)


def _rope_tables(seq_len):
    rows = seq_len // GRID_W
    row = jnp.repeat(jnp.arange(rows), GRID_W).astype(F32)
    col = jnp.tile(jnp.arange(GRID_W), rows).astype(F32)
    n_freq = HEAD_DIM // 4
    inv_freq = ROPE_THETA ** (-jnp.arange(n_freq, dtype=F32) / n_freq)
    ang_r = row[:, None] * inv_freq[None, :]
    ang_c = col[:, None] * inv_freq[None, :]
    ang = jnp.concatenate([ang_r, ang_r, ang_c, ang_c], axis=-1)
    reps = V7X_LANES // HEAD_DIM
    cos = jnp.tile(jnp.cos(ang), (1, reps))
    sin = jnp.tile(jnp.sin(ang), (1, reps))
    even = (np.arange(V7X_LANES) // n_freq) % 2 == 0
    return cos, jnp.where(even, -sin, 0.0), jnp.where(even, 0.0, sin)


def kernel(x, c, ctx, c_ctx, w_ada, b_ada, g_pre_mix, w_in, sink, w_sgu, b_sgu, g_sgu,
           g_mix, w_out, g_post_mix, g_pre_ff, w_ff1, w_ff2, g_post_ff):
    b, s, d = x.shape
    ctx_len = ctx.shape[1]
    depth = w_ada.shape[0]
    n_q = sink.shape[1]
    aw = n_q * HEAD_DIM
    n_sgu = w_sgu.shape[1]
    sw = n_sgu * GROUP_DIM
    fw = g_mix.shape[1] - aw - sw
    kvw = (w_in.shape[2] - aw - 2 * sw - fw) // 2
    n_kv = kvw // HEAD_DIM
    grp = n_q // n_kv
    assert n_kv == 2 and kvw == V7X_LANES and s % (8 * BLOCK) == 0 and ctx_len % BLOCK == 0
    assert s % (V7X_SUBLANES * FFT_INNER) == 0 and fw % V7X_LANES == 0
    widths = (aw, kvw, sw, fw)

    rows = -(-(b + 1) // 8) * 8
    cc = jnp.zeros((rows, d), F32).at[:b].set(c).at[b].set(c_ctx)
    mod = _modulation(cc, w_ada, b_ada).reshape(depth, rows, 1, 6 * d)

    rope_x = _rope_tables(s)
    rope_c = (jnp.ones((ctx_len, V7X_LANES), F32), jnp.zeros((ctx_len, V7X_LANES), F32),
              jnp.zeros((ctx_len, V7X_LANES), F32))
    ones = _mxu_const(np.kron(np.eye(n_sgu), np.ones((GROUP_DIM, GROUP_DIM))))
    m1, w3 = _dft_tables(s, FFT_INNER)
    cs_np = _channel_dft(fw)
    cs = _mxu_const(cs_np)
    cs_t = _mxu_const(np.concatenate([cs_np[:fw], cs_np[fw:]], axis=1))
    kl = np.arange(ctx_len)
    al = 2.0 * np.pi * ((kl[:, None] * kl[None, :]) % ctx_len) / ctx_len
    dl = _mxu_const(np.concatenate([np.cos(al), -np.sin(al)], axis=1))

    lane = np.arange(aw)
    perm = np.where(lane % V7X_LANES < HEAD_DIM,
                    (lane // V7X_LANES) * HEAD_DIM + lane % V7X_LANES,
                    (grp + lane // V7X_LANES) * HEAD_DIM + lane % V7X_LANES - HEAD_DIM)

    row2 = lambda v: v.reshape(1, -1)
    h_ctx = ctx
    for l in range(depth):
        last = l == depth - 1
        wsgu = w_sgu[l].reshape(n_sgu * CHUNK, CHUNK).astype(BF16)
        bsgu = jnp.repeat(b_sgu[l].T, GROUP_DIM, axis=1)
        gsgu = g_sgu[l].reshape(1, sw)
        gmix = row2(jnp.concatenate([g_mix[l][:aw][perm], g_mix[l][aw:]]))
        wo = jnp.concatenate([w_out[l][:aw][perm], w_out[l][aw:]], axis=0).astype(BF16)
        shared = (row2(g_pre_mix[l]), w_in[l].astype(BF16))
        sgu_w = (wsgu, bsgu, gsgu, ones)
        ff_w = (gmix, row2(g_post_mix[l]), row2(g_pre_ff[l]), row2(g_post_ff[l]), wo,
                w_ff1[l].astype(BF16), w_ff2[l].astype(BF16))

        cq, ckd, cvt, csg, cf = _in_proj(h_ctx, mod, l, b, *shared, rope_c, *sgu_w, ctx_len, widths,
                                         fft_layout=False)
        if not last:
            ca = _attention(sink[l], cq, None, None, ckd, cvt, ctx_len, local=False)
            cfo = _fourier_ctx(cf, cs_t, dl)
            h_ctx = _out_ff(h_ctx, ca, csg, cfo, mod, l, b, *ff_w, ctx_len)

        q, kd, vt, sg, f = _in_proj(x, mod, l, None, *shared, rope_x, *sgu_w, 8 * BLOCK, widths,
                                    fft_layout=True)
        a = _attention(sink[l], q, kd, vt, ckd, cvt, 8 * BLOCK, local=True)
        fo = _fourier_latent(f, m1, w3, cs)
        x = _out_ff(x, a, sg, fo, mod, l, None, *ff_w, 8 * BLOCK)
    return x
```

```python
import functools
import math

import numpy as np
import jax
import jax.numpy as jnp
from jax import lax
from jax.experimental import pallas as pl
from jax.experimental.pallas import tpu as pltpu

F32 = jnp.float32
BF16 = jnp.bfloat16

HEAD_DIM = 64
GRID_W = 64
BLOCK = 128
CHUNK = 128
GROUP_DIM = 64
ROPE_THETA = 10000.0
EPS = 1e-6
NEG_INF = -1e30
LOG2E = math.log2(math.e)
MOD_SHIFT1, MOD_SCALE1, MOD_GATE1, MOD_SHIFT2, MOD_SCALE2, MOD_GATE2 = range(6)
FFT_INNER = 128
SCORE_LOOKAHEAD = 2

V7X_LANES = 128
V7X_SUBLANES = 8
V7X_VMEM_BYTES = 64 * 2**20
VMEM_LIMIT = V7X_VMEM_BYTES * 3 // 4


def _params(*sem):
    return pltpu.CompilerParams(dimension_semantics=sem, vmem_limit_bytes=VMEM_LIMIT)


def _resident(shape):
    return pl.BlockSpec(shape, lambda *_: (0,) * len(shape), pipeline_mode=pl.Buffered(1))


def _rms(x):
    return x * lax.rsqrt(jnp.mean(x * x, axis=-1, keepdims=True) + EPS)


def _gelu_tanh(x):
    c = math.sqrt(2.0 / math.pi)
    return x * (0.5 + 0.5 * jnp.tanh(x * (c + (c * 0.044715) * (x * x))))


def _dot(a, b):
    return jnp.dot(a, b, preferred_element_type=F32)


def _dot_nt(a, b):
    return lax.dot_general(a, b, (((1,), (1,)), ((), ())), preferred_element_type=F32)


def _mod_kernel(cc_ref, w_ref, b_ref, o_ref):
    cc = cc_ref[...]
    s = cc / (1.0 + jnp.exp(-cc))
    o_ref[0] = _dot(s.astype(BF16), w_ref[0].astype(BF16)) + b_ref[0]


def _modulation(cc, w_ada, b_ada):
    depth, d, n = w_ada.shape
    rows = cc.shape[0]
    tn = n // 4
    return pl.pallas_call(
        _mod_kernel,
        grid=(depth, n // tn),
        in_specs=[
            pl.BlockSpec((rows, d), lambda l, j: (0, 0)),
            pl.BlockSpec((1, d, tn), lambda l, j: (l, 0, j)),
            pl.BlockSpec((1, 1, tn), lambda l, j: (l, 0, j)),
        ],
        out_specs=pl.BlockSpec((1, rows, tn), lambda l, j: (l, 0, j)),
        out_shape=jax.ShapeDtypeStruct((depth, rows, n), F32),
        compiler_params=_params("arbitrary", "arbitrary"),
        name="modulation",
    )(cc, w_ada, b_ada.reshape(depth, 1, n))


def _in_proj_kernel(x_ref, sc_ref, sh_ref, g_ref, w_ref, cos_ref, sa_ref, sb_ref,
                    wsgu_ref, bsgu_ref, gsgu_ref, ones_ref,
                    q_ref, kd_ref, v_ref, sgu_ref, f_ref, *, widths, row_chunk, fft_layout):
    aw, kvw, sw, fw = widths
    n_rows = x_ref.shape[1] // row_chunk
    scale = HEAD_DIM ** -0.5 * LOG2E
    n_heads = sw // GROUP_DIM
    lane = lax.broadcasted_iota(jnp.int32, (CHUNK, sw), 1)

    gain = g_ref[...] * (1.0 + sc_ref[0, 0])
    shift = sh_ref[0, 0]

    def modulated(r):
        rows = slice(r * row_chunk, (r + 1) * row_chunk)
        return (_rms(x_ref[0, rows, :]) * gain + shift).astype(BF16)

    def split(r, px):
        base = r * row_chunk
        rows = slice(base, base + row_chunk)
        cos, sa, sb = cos_ref[rows, :], sa_ref[rows, :], sb_ref[rows, :]

        def rope(t):
            return t * cos + pltpu.roll(t, V7X_LANES - 16, 1) * sa + pltpu.roll(t, 16, 1) * sb

        for j in range(aw // V7X_LANES):
            qj = rope(px[:, j * V7X_LANES:(j + 1) * V7X_LANES]) * scale
            q_ref[0, rows, j * V7X_LANES:(j + 1) * V7X_LANES] = qj.astype(BF16)

        k = rope(px[:, aw:aw + kvw])
        kr = pltpu.roll(k, HEAD_DIM, 1)
        lo = lax.broadcasted_iota(jnp.int32, k.shape, 1) < HEAD_DIM
        kd_ref[0, rows, :V7X_LANES] = jnp.where(lo, k, kr).astype(BF16)
        kd_ref[0, rows, V7X_LANES:] = jnp.where(lo, kr, k).astype(BF16)
        vt = px[:, aw + kvw:aw + 2 * kvw].T
        head0 = lax.broadcasted_iota(jnp.int32, vt.shape, 0) < HEAD_DIM
        v_ref[0, :kvw, rows] = jnp.where(head0, vt, 1.0).astype(BF16)
        v_ref[0, kvw:, rows] = jnp.where(head0, 1.0, vt).astype(BF16)

        o = aw + 2 * kvw
        u = _gelu_tanh(px[:, o:o + sw])
        gv = _gelu_tanh(px[:, o + sw:o + 2 * sw])
        ss = gv * gv
        hi = ss.astype(BF16)
        lo2 = (ss - hi.astype(F32)).astype(BF16)
        gsum = _dot(hi, ones_ref[...]) + _dot(lo2, ones_ref[...])
        vn = (gv * lax.rsqrt(gsum * (1.0 / GROUP_DIM) + EPS) * gsgu_ref[...]).astype(BF16)
        for c in range(row_chunk // CHUNK):
            crow = slice(c * CHUNK, (c + 1) * CHUNK)
            m = _dot(wsgu_ref[...], vn[crow])
            mixed = m[(n_heads - 1) * CHUNK:]
            for h in range(n_heads - 2, -1, -1):
                mixed = jnp.where(lane < (h + 1) * GROUP_DIM, m[h * CHUNK:(h + 1) * CHUNK], mixed)
            sgu_ref[0, base + c * CHUNK:base + (c + 1) * CHUNK, :] = u[crow] * (mixed + bsgu_ref[...])

        fcols = px[:, o + 2 * sw:]
        if fft_layout:
            for c in range(row_chunk // FFT_INNER):
                f_ref[0, :, base // FFT_INNER + c, :] = fcols[c * FFT_INNER:(c + 1) * FFT_INNER]
        else:
            f_ref[0, rows, :] = fcols.astype(BF16)

    nxt = _dot(modulated(0), w_ref[...])
    for r in range(n_rows):
        px = nxt
        if r + 1 < n_rows:
            nxt = _dot(modulated(r + 1), w_ref[...])
        split(r, px)


def _mod_spec(d, layer, row, part):
    return pl.BlockSpec((1, 1, 1, d), lambda bi, i: (layer, bi if row is None else row, 0, part))


def _in_proj(x, mod, layer, mod_row, g, w_in, rope_tabs, wsgu, bsgu, gsgu, ones, tm, widths, fft_layout):
    b, t, d = x.shape
    aw, kvw, sw, fw = widths
    n_in = w_in.shape[1]
    tok = lambda w: pl.BlockSpec((1, tm, w), lambda bi, i: (bi, i, 0))
    vec = lambda part: _mod_spec(d, layer, mod_row, part)
    tab = pl.BlockSpec((tm, V7X_LANES), lambda bi, i: (i, 0))
    if fft_layout:
        f_spec = pl.BlockSpec((1, FFT_INNER, tm // FFT_INNER, fw), lambda bi, i: (bi, 0, i, 0))
        f_shape = jax.ShapeDtypeStruct((b, FFT_INNER, t // FFT_INNER, fw), F32)
    else:
        f_spec, f_shape = tok(fw), jax.ShapeDtypeStruct((b, t, fw), BF16)
    return pl.pallas_call(
        functools.partial(_in_proj_kernel, widths=widths, row_chunk=min(tm, 4 * BLOCK),
                          fft_layout=fft_layout),
        grid=(b, t // tm),
        in_specs=[tok(d), vec(MOD_SCALE1), vec(MOD_SHIFT1), _resident((1, d)), _resident((d, n_in)),
                  tab, tab, tab,
                  _resident(wsgu.shape), _resident(bsgu.shape), _resident(gsgu.shape),
                  _resident(ones.shape)],
        out_specs=[tok(aw), tok(2 * kvw), pl.BlockSpec((1, 2 * kvw, tm), lambda bi, i: (bi, 0, i)),
                   tok(sw), f_spec],
        out_shape=[jax.ShapeDtypeStruct((b, t, aw), BF16),
                   jax.ShapeDtypeStruct((b, t, 2 * kvw), BF16),
                   jax.ShapeDtypeStruct((b, 2 * kvw, t), BF16),
                   jax.ShapeDtypeStruct((b, t, sw), F32),
                   f_shape],
        compiler_params=_params("arbitrary", "arbitrary"),
        name="in_proj",
    )(x, mod, mod, g, w_in, *rope_tabs, wsgu, bsgu, gsgu, ones)


def _attn_kernel(sink_ref, q_ref, *refs, local):
    if local:
        kp_ref, ko_ref, kn_ref, vp_ref, vo_ref, vn_ref, ck_ref, cv_ref, o_ref = refs
    else:
        ck_ref, cv_ref, o_ref = refs
    tq = q_ref.shape[1]
    kvw = ck_ref.shape[2] // 2
    n_kv = kvw // HEAD_DIM
    grp = q_ref.shape[2] // HEAD_DIM // n_kv
    cols = grp * BLOCK
    lane = lax.broadcasted_iota(jnp.int32, (BLOCK, V7X_LANES), 1)
    head_lane = lax.broadcasted_iota(jnp.int32, (1, cols), 1)
    kv_row = lax.broadcasted_iota(jnp.int32, (kvw, cols), 0)
    if local:
        vt_all = jnp.concatenate([vp_ref[0], vo_ref[0], vn_ref[0]], axis=1)
        key = lax.broadcasted_iota(jnp.int32, (BLOCK, cols), 0)
        qpos = lax.broadcasted_iota(jnp.int32, (BLOCK, cols), 1) & (BLOCK - 1)
        ok_prev = key >= qpos
        ok_next = key <= qpos
        not_first = pl.program_id(1) > 0
        not_last = pl.program_id(1) < pl.num_programs(1) - 1
    colmax = lambda t: jnp.max(t, axis=0, keepdims=True)

    n_sb = tq // BLOCK

    def scores(sb, h):
        qrows = slice(sb * BLOCK, (sb + 1) * BLOCK)
        kcols = slice(h * V7X_LANES, (h + 1) * V7X_LANES)
        parts = []
        for g in range(grp):
            head = h * grp + g
            qc = q_ref[0, qrows, (head // 2) * V7X_LANES:(head // 2 + 1) * V7X_LANES]
            keep = (lane < HEAD_DIM) if head % 2 == 0 else (lane >= HEAD_DIM)
            parts.append(jnp.where(keep, qc, jnp.zeros_like(qc)))
        qs = jnp.concatenate(parts, axis=0)
        tiles = [_dot_nt(ck_ref[0, :, kcols], qs)]
        if local:
            k_win = jnp.concatenate(
                [kp_ref[0, :, kcols], ko_ref[0, :, kcols], kn_ref[0, :, kcols]],
                axis=0)[sb * BLOCK:(sb + 3) * BLOCK]
            s_loc = _dot_nt(k_win, qs)
            s_prev = jnp.where(ok_prev, s_loc[:BLOCK], NEG_INF)
            s_next = jnp.where(ok_next, s_loc[2 * BLOCK:], NEG_INF)
            if sb == 0:
                s_prev = jnp.where(not_first, s_prev, NEG_INF)
            if sb == n_sb - 1:
                s_next = jnp.where(not_last, s_next, NEG_INF)
            tiles += [s_prev, s_loc[BLOCK:2 * BLOCK], s_next]
        return tiles

    def attend(sb, h, tiles):
        vrows = slice(h * kvw, (h + 1) * kvw)
        ones_row = ((h + 1) % n_kv) * HEAD_DIM
        sk = jnp.full((1, cols), sink_ref[h * grp + grp - 1] * LOG2E, F32)
        for g in range(grp - 2, -1, -1):
            sk = jnp.where(head_lane < (g + 1) * BLOCK, sink_ref[h * grp + g] * LOG2E, sk)
        m = sk
        for t in tiles:
            m = jnp.maximum(m, colmax(t))
        acc = _dot(cv_ref[0, vrows, :], jnp.exp2(tiles[0] - m).astype(BF16))
        if local:
            e_loc = jnp.concatenate([jnp.exp2(t - m) for t in tiles[1:]], axis=0)
            acc = acc + _dot(vt_all[vrows, sb * BLOCK:(sb + 3) * BLOCK], e_loc.astype(BF16))
        den = acc[ones_row:ones_row + 1, :] + jnp.exp2(sk - m)
        return acc / den

    units = [(sb, h) for sb in range(n_sb) for h in range(n_kv)]
    pending = [scores(*un) for un in units[:SCORE_LOOKAHEAD]]
    outs = []
    for u, (sb, h) in enumerate(units):
        tiles = pending.pop(0)
        if u + SCORE_LOOKAHEAD < len(units):
            pending.append(scores(*units[u + SCORE_LOOKAHEAD]))
        outs.append(attend(sb, h, tiles))
        if h == n_kv - 1:
            o_t = jnp.where(kv_row < HEAD_DIM, outs[0], outs[1])
            outs = []
            for g in range(grp):
                o_ref[0, sb * BLOCK:(sb + 1) * BLOCK, g * V7X_LANES:(g + 1) * V7X_LANES] = (
                    o_t[:, g * BLOCK:(g + 1) * BLOCK].T.astype(BF16))


def _attention(sink, q, kd, vt, ckd, cvt, tq, local):
    b, t, aw = q.shape
    ctx_len = ckd.shape[1]
    kvw = cvt.shape[1] // 2
    nsb = tq // BLOCK
    nb = t // BLOCK
    smem = pl.BlockSpec(memory_space=pltpu.SMEM)
    qspec = pl.BlockSpec((1, tq, aw), lambda bi, i: (bi, i, 0))
    in_specs = [smem, qspec]
    args = [sink, q]
    if local:
        prev = lambda bi, i: (bi, jnp.maximum(i * nsb - 1, 0))
        nxt = lambda bi, i: (bi, jnp.minimum((i + 1) * nsb, nb - 1))
        in_specs += [pl.BlockSpec((1, BLOCK, 2 * kvw), lambda bi, i: (*prev(bi, i), 0)),
                     pl.BlockSpec((1, tq, 2 * kvw), lambda bi, i: (bi, i, 0)),
                     pl.BlockSpec((1, BLOCK, 2 * kvw), lambda bi, i: (*nxt(bi, i), 0)),
                     pl.BlockSpec((1, 2 * kvw, BLOCK), lambda bi, i: (bi, 0, prev(bi, i)[1])),
                     pl.BlockSpec((1, 2 * kvw, tq), lambda bi, i: (bi, 0, i)),
                     pl.BlockSpec((1, 2 * kvw, BLOCK), lambda bi, i: (bi, 0, nxt(bi, i)[1]))]
        args += [kd, kd, kd, vt, vt, vt]
    in_specs += [pl.BlockSpec((1, ctx_len, 2 * kvw), lambda bi, i: (bi, 0, 0)),
                 pl.BlockSpec((1, 2 * kvw, ctx_len), lambda bi, i: (0, 0, bi))]
    args += [ckd, cvt]
    return pl.pallas_call(
        functools.partial(_attn_kernel, local=local),
        grid=(b, t // tq),
        in_specs=in_specs,
        out_specs=qspec,
        out_shape=jax.ShapeDtypeStruct((b, t, aw), BF16),
        compiler_params=_params("arbitrary", "arbitrary"),
        name="attention_local" if local else "attention_ctx",
    )(*args)


def _mxu_const(a):
    return jnp.asarray(a, F32).astype(BF16)


def _dft_tables(n_seq, n2):
    n1 = n_seq // n2
    s2 = np.arange(n2)[:, None, None]
    k1 = np.arange(n1)[None, :, None]
    s1 = np.arange(n1)[None, None, :]
    ang = 2.0 * np.pi * ((k1 * (n2 * s1 + s2)) % n_seq) / n_seq
    m1 = np.concatenate([np.cos(ang), -np.sin(ang)], axis=1)
    a2 = 2.0 * np.pi * ((np.arange(n2)[:, None] * np.arange(n2)[None, :]) % n2) / n2
    c2, s2m = np.cos(a2), np.sin(a2)
    w3 = np.block([[c2, s2m], [-s2m, c2]])
    return _mxu_const(m1), _mxu_const(w3)


def _channel_dft(width):
    d = np.arange(GROUP_DIM)
    a = 2.0 * np.pi * ((d[:, None] * d[None, :]) % GROUP_DIM) / GROUP_DIM
    eye = np.eye(width // GROUP_DIM)
    return np.concatenate([np.kron(eye, np.cos(a)), np.kron(eye, np.sin(a))], axis=0)


def _fft_kernel(x_ref, m_ref, w_ref, cs_ref, o_ref, a_scr, *, n_stage1, pitch, scale):
    i = pl.program_id(1)
    t2, n1, fw = x_ref.shape[1:]
    n2 = w_ref.shape[0] // 2
    t1 = o_ref.shape[2]
    slabs = fw // V7X_LANES

    @pl.when(i < n_stage1)
    def _stage1():
        for j in range(t2):
            a = _dot(m_ref[j], x_ref[0, j].astype(BF16))
            row0 = pl.multiple_of((i * t2 + j) * pitch, V7X_SUBLANES)
            for sl in range(slabs):
                a_scr[sl, pl.ds(row0, 2 * n1), :] = a[:, sl * V7X_LANES:(sl + 1) * V7X_LANES]

    @pl.when(i >= n_stage1)
    def _stage2():
        for t in range(t1):
            k1 = (i - n_stage1) * t1 + t

            def over_s2(row):
                return jnp.concatenate(
                    [a_scr[sl, pl.ds(row, n2, stride=pitch), :] for sl in range(slabs)], axis=1)

            x = jnp.concatenate([over_s2(k1), over_s2(n1 + k1)], axis=0).astype(BF16)
            z = _dot(w_ref[...], x)
            zc = jnp.concatenate([z[:n2], z[n2:]], axis=1).astype(BF16)
            o_ref[0, :, t, :] = _dot(zc, cs_ref[...]) * scale


def _fourier_latent(ft, m1, w3, cs):
    b, n2, n1, fw = ft.shape
    s = n1 * n2
    t2, t1 = 32, 32
    n_stage1, n_stage2 = n2 // t2, n1 // t1
    tile1 = lambda i: jnp.minimum(i, n_stage1 - 1)
    pitch = 2 * n1 + V7X_SUBLANES
    y = pl.pallas_call(
        functools.partial(_fft_kernel, n_stage1=n_stage1, pitch=pitch,
                          scale=1.0 / math.sqrt(s * GROUP_DIM)),
        grid=(b, n_stage1 + n_stage2),
        in_specs=[pl.BlockSpec((1, t2, n1, fw), lambda bi, i: (bi, tile1(i), 0, 0)),
                  pl.BlockSpec((t2, 2 * n1, n1), lambda bi, i: (tile1(i), 0, 0)),
                  _resident(w3.shape), _resident(cs.shape)],
        out_specs=pl.BlockSpec((1, n2, t1, fw), lambda bi, i: (bi, 0, jnp.maximum(i - n_stage1, 0), 0)),
        out_shape=jax.ShapeDtypeStruct((b, n2, n1, fw), F32),
        scratch_shapes=[pltpu.VMEM((fw // V7X_LANES, n2 * pitch, V7X_LANES), F32)],
        compiler_params=_params("arbitrary", "arbitrary"),
        name="fft",
    )(ft, m1, w3, cs)
    return y.reshape(b, s, fw)


def _fourier_ctx_kernel(f_ref, cs_ref, dl_ref, o_ref, *, scale):
    fw = f_ref.shape[2]
    fc = _dot(f_ref[0], cs_ref[...])
    stacked = jnp.concatenate([fc[:, :fw], fc[:, fw:]], axis=0).astype(BF16)
    o_ref[0] = _dot(dl_ref[...], stacked) * scale


def _fourier_ctx(f, cs_t, dl):
    b, l, fw = f.shape
    return pl.pallas_call(
        functools.partial(_fourier_ctx_kernel, scale=1.0 / math.sqrt(l * GROUP_DIM)),
        grid=(b,),
        in_specs=[pl.BlockSpec((1, l, fw), lambda bi: (bi, 0, 0)),
                  _resident(cs_t.shape), _resident(dl.shape)],
        out_specs=pl.BlockSpec((1, l, fw), lambda bi: (bi, 0, 0)),
        out_shape=jax.ShapeDtypeStruct((b, l, fw), F32),
        compiler_params=_params("arbitrary"),
        name="fourier_ctx",
    )(f, cs_t, dl)


def _out_ff_kernel(x_ref, a_ref, s_ref, f_ref, g1_ref, sc2_ref, sh2_ref, g2_ref,
                   gmix_ref, gpm_ref, gpf_ref, gpo_ref, wo_ref, w1_ref, w2_ref, o_ref, *,
                   ff_chunk, row_chunks):
    aw = a_ref.shape[2]
    sw = s_ref.shape[2]
    gm = gmix_ref[...]
    d_ff = w1_ref.shape[1]
    n_rows = len(row_chunks)
    starts = [sum(row_chunks[:r]) for r in range(n_rows + 1)]
    gate1 = g1_ref[0, 0] * gpm_ref[...]
    gain2 = gpf_ref[...] * (1.0 + sc2_ref[0, 0])
    shift2 = sh2_ref[0, 0]
    gate2 = g2_ref[0, 0] * gpo_ref[...]

    def mix(r):
        rows = slice(starts[r], starts[r + 1])
        ycat = jnp.concatenate([
            (_rms(a_ref[0, rows, :].astype(F32)) * gm[:, :aw]).astype(BF16),
            (_rms(s_ref[0, rows, :]) * gm[:, aw:aw + sw]).astype(BF16),
            (_rms(f_ref[0, rows, :]) * gm[:, aw + sw:]).astype(BF16)], axis=1)
        y = _dot(ycat, wo_ref[...])
        x1 = x_ref[0, rows, :] + _rms(y) * gate1
        return x1, (_rms(x1) * gain2 + shift2).astype(BF16)

    n_ff = d_ff // ff_chunk

    def up(h2, c):
        hid = jnp.maximum(_dot(h2, w1_ref[:, c * ff_chunk:(c + 1) * ff_chunk]), 0.0)
        return (hid * hid).astype(BF16)

    def down(hid, c):
        return _dot(hid, w2_ref[c * ff_chunk:(c + 1) * ff_chunk, :])

    nxt = mix(0)
    for r in range(n_rows):
        x1, h2 = nxt
        hid = up(h2, 0)
        ff = None
        for c in range(n_ff):
            hid_next = up(h2, c + 1) if c + 1 < n_ff else None
            if c == 0 and r + 1 < n_rows:
                nxt = mix(r + 1)
            part = down(hid, c)
            ff = part if ff is None else ff + part
            hid = hid_next
        o_ref[0, starts[r]:starts[r + 1], :] = x1 + _rms(ff) * gate2


def _out_ff(x, a, sg, fo, mod, layer, mod_row, gmix, gpm, gpf, gpo, wo, w1, w2, tm):
    b, t, d = x.shape
    tok = lambda w: pl.BlockSpec((1, tm, w), lambda bi, i: (bi, i, 0))
    vec = lambda part: _mod_spec(d, layer, mod_row, part)
    row_chunks = (min(tm, 4 * BLOCK),) * (tm // min(tm, 4 * BLOCK))
    return pl.pallas_call(
        functools.partial(_out_ff_kernel, ff_chunk=min(w1.shape[1], 1024), row_chunks=row_chunks),
        grid=(b, t // tm),
        in_specs=[tok(d), tok(a.shape[2]), tok(sg.shape[2]), tok(fo.shape[2]),
                  vec(MOD_GATE1), vec(MOD_SCALE2), vec(MOD_SHIFT2), vec(MOD_GATE2),
                  _resident(gmix.shape), _resident(gpm.shape), _resident(gpf.shape), _resident(gpo.shape),
                  _resident(wo.shape), _resident(w1.shape), _resident(w2.shape)],
        out_specs=tok(d),
        out_shape=jax.ShapeDtypeStruct((b, t, d), F32),
        compiler_params=_params("arbitrary", "arbitrary"),
        name="out_ff",
    )(x, a, sg, fo, mod, mod, mod, mod, gmix, gpm, gpf, gpo, wo, w1, w2)


def _rope_tables(seq_len):
    rows = seq_len // GRID_W
    row = jnp.repeat(jnp.arange(rows), GRID_W).astype(F32)
    col = jnp.tile(jnp.arange(GRID_W), rows).astype(F32)
    n_freq = HEAD_DIM // 4
    inv_freq = ROPE_THETA ** (-jnp.arange(n_freq, dtype=F32) / n_freq)
    ang_r = row[:, None] * inv_freq[None, :]
    ang_c = col[:, None] * inv_freq[None, :]
    ang = jnp.concatenate([ang_r, ang_r, ang_c, ang_c], axis=-1)
    reps = V7X_LANES // HEAD_DIM
    cos = jnp.tile(jnp.cos(ang), (1, reps))
    sin = jnp.tile(jnp.sin(ang), (1, reps))
    even = (np.arange(V7X_LANES) // n_freq) % 2 == 0
    return cos, jnp.where(even, -sin, 0.0), jnp.where(even, 0.0, sin)


def kernel(x, c, ctx, c_ctx, w_ada, b_ada, g_pre_mix, w_in, sink, w_sgu, b_sgu, g_sgu,
           g_mix, w_out, g_post_mix, g_pre_ff, w_ff1, w_ff2, g_post_ff):
    b, s, d = x.shape
    ctx_len = ctx.shape[1]
    depth = w_ada.shape[0]
    n_q = sink.shape[1]
    aw = n_q * HEAD_DIM
    n_sgu = w_sgu.shape[1]
    sw = n_sgu * GROUP_DIM
    fw = g_mix.shape[1] - aw - sw
    kvw = (w_in.shape[2] - aw - 2 * sw - fw) // 2
    n_kv = kvw // HEAD_DIM
    grp = n_q // n_kv
    assert n_kv == 2 and kvw == V7X_LANES and s % (8 * BLOCK) == 0 and ctx_len % BLOCK == 0
    assert s % (V7X_SUBLANES * FFT_INNER) == 0 and fw % V7X_LANES == 0
    widths = (aw, kvw, sw, fw)

    rows = -(-(b + 1) // 8) * 8
    cc = jnp.zeros((rows, d), F32).at[:b].set(c).at[b].set(c_ctx)
    mod = _modulation(cc, w_ada, b_ada).reshape(depth, rows, 1, 6 * d)

    rope_x = _rope_tables(s)
    n_ctx = b * ctx_len
    ctx_tile = 8 * BLOCK if n_ctx % (8 * BLOCK) == 0 else ctx_len
    rope_c = (jnp.ones((n_ctx, V7X_LANES), F32), jnp.zeros((n_ctx, V7X_LANES), F32),
              jnp.zeros((n_ctx, V7X_LANES), F32))
    ones = _mxu_const(np.kron(np.eye(n_sgu), np.ones((GROUP_DIM, GROUP_DIM))))
    m1, w3 = _dft_tables(s, FFT_INNER)
    cs_np = _channel_dft(fw)
    cs = _mxu_const(cs_np)
    cs_t = _mxu_const(np.concatenate([cs_np[:fw], cs_np[fw:]], axis=1))
    kl = np.arange(ctx_len)
    al = 2.0 * np.pi * ((kl[:, None] * kl[None, :]) % ctx_len) / ctx_len
    dl = _mxu_const(np.concatenate([np.cos(al), -np.sin(al)], axis=1))

    lane = np.arange(aw)
    perm = np.where(lane % V7X_LANES < HEAD_DIM,
                    (lane // V7X_LANES) * HEAD_DIM + lane % V7X_LANES,
                    (grp + lane // V7X_LANES) * HEAD_DIM + lane % V7X_LANES - HEAD_DIM)

    row2 = lambda v: v.reshape(1, -1)
    h_ctx = ctx.reshape(1, n_ctx, d)
    for l in range(depth):
        last = l == depth - 1
        wsgu = w_sgu[l].reshape(n_sgu * CHUNK, CHUNK).astype(BF16)
        bsgu = jnp.repeat(b_sgu[l].T, GROUP_DIM, axis=1)
        gsgu = g_sgu[l].reshape(1, sw)
        gmix = row2(jnp.concatenate([g_mix[l][:aw][perm], g_mix[l][aw:]]))
        wo = jnp.concatenate([w_out[l][:aw][perm], w_out[l][aw:]], axis=0).astype(BF16)
        shared = (row2(g_pre_mix[l]), w_in[l].astype(BF16))
        sgu_w = (wsgu, bsgu, gsgu, ones)
        ff_w = (gmix, row2(g_post_mix[l]), row2(g_pre_ff[l]), row2(g_post_ff[l]), wo,
                w_ff1[l].astype(BF16), w_ff2[l].astype(BF16))

        cq, ckd, cvt, csg, cf = _in_proj(h_ctx, mod, l, b, *shared, rope_c, *sgu_w, ctx_tile, widths,
                                         fft_layout=False)
        ckd = ckd.reshape(b, ctx_len, 2 * kvw)
        if not last:
            ca = _attention(sink[l], cq.reshape(b, ctx_len, aw), None, None, ckd, cvt, ctx_len, local=False)
            cfo = _fourier_ctx(cf.reshape(b, ctx_len, fw), cs_t, dl)
            h_ctx = _out_ff(h_ctx, ca.reshape(1, b * ctx_len, aw), csg, cfo.reshape(1, b * ctx_len, fw),
                            mod, l, b, *ff_w, ctx_tile)

        q, kd, vt, sg, f = _in_proj(x, mod, l, None, *shared, rope_x, *sgu_w, 8 * BLOCK, widths,
                                    fft_layout=True)
        a = _attention(sink[l], q, kd, vt, ckd, cvt, 16 * BLOCK, local=True)
        fo = _fourier_latent(f, m1, w3, cs)
        x = _out_ff(x, a, sg, fo, mod, l, None, *ff_w, 8 * BLOCK)
    return x
```

```python
import functools
import math

import numpy as np
import jax
import jax.numpy as jnp
from jax import lax
from jax.experimental import pallas as pl
from jax.experimental.pallas import tpu as pltpu

F32 = jnp.float32
BF16 = jnp.bfloat16

HEAD_DIM = 64
GRID_W = 64
BLOCK = 128
CHUNK = 128
GROUP_DIM = 64
ROPE_THETA = 10000.0
EPS = 1e-6
NEG_INF = -1e30
LOG2E = math.log2(math.e)
QUERY_SCALE = HEAD_DIM ** -0.5 * LOG2E
MOD_SHIFT1, MOD_SCALE1, MOD_GATE1, MOD_SHIFT2, MOD_SCALE2, MOD_GATE2 = range(6)
FFT_INNER = 128
SCORE_LOOKAHEAD = 2

V7X_LANES = 128
V7X_SUBLANES = 8
V7X_BF16_SUBLANES = 16
VT_ROWS = HEAD_DIM + V7X_BF16_SUBLANES
V7X_VMEM_BYTES = 64 * 2**20
VMEM_LIMIT = V7X_VMEM_BYTES * 3 // 4


def _params(*sem):
    return pltpu.CompilerParams(dimension_semantics=sem, vmem_limit_bytes=VMEM_LIMIT)


def _resident(shape):
    return pl.BlockSpec(shape, lambda *_: (0,) * len(shape), pipeline_mode=pl.Buffered(1))


def _rms(x):
    return x * lax.rsqrt(jnp.mean(x * x, axis=-1, keepdims=True) + EPS)


def _gelu_tanh(x):
    c = math.sqrt(2.0 / math.pi)
    return x * (0.5 + 0.5 * jnp.tanh(x * (c + (c * 0.044715) * (x * x))))


def _dot(a, b):
    return jnp.dot(a, b, preferred_element_type=F32)


def _dot_nt(a, b):
    return lax.dot_general(a, b, (((1,), (1,)), ((), ())), preferred_element_type=F32)


def _mod_kernel(cc_ref, w_ref, b_ref, o_ref):
    cc = cc_ref[...]
    s = cc / (1.0 + jnp.exp(-cc))
    o_ref[0] = _dot(s.astype(BF16), w_ref[0].astype(BF16)) + b_ref[0]


def _modulation(cc, w_ada, b_ada):
    depth, d, n = w_ada.shape
    rows = cc.shape[0]
    tn = n // 4
    return pl.pallas_call(
        _mod_kernel,
        grid=(depth, n // tn),
        in_specs=[
            pl.BlockSpec((rows, d), lambda l, j: (0, 0)),
            pl.BlockSpec((1, d, tn), lambda l, j: (l, 0, j)),
            pl.BlockSpec((1, 1, tn), lambda l, j: (l, 0, j)),
        ],
        out_specs=pl.BlockSpec((1, rows, tn), lambda l, j: (l, 0, j)),
        out_shape=jax.ShapeDtypeStruct((depth, rows, n), F32),
        compiler_params=_params("arbitrary", "arbitrary"),
        name="modulation",
    )(cc, w_ada, b_ada.reshape(depth, 1, n))


def _in_proj_kernel(x_ref, sc_ref, sh_ref, g_ref, w_ref, cos_ref, sa_ref, sb_ref,
                    wsgu_ref, bsgu_ref, gsgu_ref, ones_ref,
                    q_ref, kd_ref, v_ref, sgu_ref, f_ref, *, widths, row_chunk, fft_layout):
    aw, kvw, sw, fw = widths
    n_rows = x_ref.shape[1] // row_chunk
    n_heads = sw // GROUP_DIM
    lane = lax.broadcasted_iota(jnp.int32, (CHUNK, sw), 1)

    gain = g_ref[...] * (1.0 + sc_ref[0, 0])
    shift = sh_ref[0, 0]

    def modulated(r):
        rows = slice(r * row_chunk, (r + 1) * row_chunk)
        return (_rms(x_ref[0, rows, :]) * gain + shift).astype(BF16)

    def split(r, px):
        base = r * row_chunk
        rows = slice(base, base + row_chunk)
        def rope(t, cols):
            return (t * cos_ref[rows, cols] + pltpu.roll(t, V7X_LANES - 16, 1) * sa_ref[rows, cols]
                    + pltpu.roll(t, 16, 1) * sb_ref[rows, cols])

        key_tabs, query_tabs = slice(0, V7X_LANES), slice(V7X_LANES, 2 * V7X_LANES)
        for j in range(aw // V7X_LANES):
            qj = rope(px[:, j * V7X_LANES:(j + 1) * V7X_LANES], query_tabs)
            q_ref[0, rows, j * V7X_LANES:(j + 1) * V7X_LANES] = qj.astype(BF16)

        k = rope(px[:, aw:aw + kvw], key_tabs)
        kr = pltpu.roll(k, HEAD_DIM, 1)
        lo = lax.broadcasted_iota(jnp.int32, k.shape, 1) < HEAD_DIM
        kd_ref[0, rows, :V7X_LANES] = jnp.where(lo, k, kr).astype(BF16)
        kd_ref[0, rows, V7X_LANES:] = jnp.where(lo, kr, k).astype(BF16)
        vt = px[:, aw + kvw:aw + 2 * kvw].T
        for h in range(kvw // HEAD_DIM):
            v_ref[0, h * VT_ROWS:h * VT_ROWS + HEAD_DIM, rows] = (
                vt[h * HEAD_DIM:(h + 1) * HEAD_DIM].astype(BF16))
            v_ref[0, h * VT_ROWS + HEAD_DIM:(h + 1) * VT_ROWS, rows] = jnp.ones(
                (VT_ROWS - HEAD_DIM, row_chunk), BF16)

        o = aw + 2 * kvw
        u = _gelu_tanh(px[:, o:o + sw])
        gv = _gelu_tanh(px[:, o + sw:o + 2 * sw])
        ss = gv * gv
        hi = ss.astype(BF16)
        lo2 = (ss - hi.astype(F32)).astype(BF16)
        gsum = _dot(hi, ones_ref[...]) + _dot(lo2, ones_ref[...])
        vn = (gv * lax.rsqrt(gsum * (1.0 / GROUP_DIM) + EPS) * gsgu_ref[...]).astype(BF16)
        for c in range(row_chunk // CHUNK):
            crow = slice(c * CHUNK, (c + 1) * CHUNK)
            m = _dot(wsgu_ref[...], vn[crow])
            mixed = m[(n_heads - 1) * CHUNK:]
            for h in range(n_heads - 2, -1, -1):
                mixed = jnp.where(lane < (h + 1) * GROUP_DIM, m[h * CHUNK:(h + 1) * CHUNK], mixed)
            sgu_ref[0, base + c * CHUNK:base + (c + 1) * CHUNK, :] = u[crow] * (mixed + bsgu_ref[...])

        fcols = px[:, o + 2 * sw:]
        if fft_layout:
            for c in range(row_chunk // FFT_INNER):
                f_ref[0, :, base // FFT_INNER + c, :] = fcols[c * FFT_INNER:(c + 1) * FFT_INNER]
        else:
            f_ref[0, rows, :] = fcols.astype(BF16)

    nxt = _dot(modulated(0), w_ref[...])
    for r in range(n_rows):
        px = nxt
        if r + 1 < n_rows:
            nxt = _dot(modulated(r + 1), w_ref[...])
        split(r, px)


def _mod_spec(d, layer, row, part):
    return pl.BlockSpec((1, 1, 1, d), lambda bi, i: (layer, bi if row is None else row, 0, part))


def _in_proj(x, mod, layer, mod_row, g, w_in, rope_tabs, wsgu, bsgu, gsgu, ones, tm, widths, fft_layout):
    b, t, d = x.shape
    aw, kvw, sw, fw = widths
    n_in = w_in.shape[1]
    n_kv = kvw // HEAD_DIM
    tok = lambda w: pl.BlockSpec((1, tm, w), lambda bi, i: (bi, i, 0))
    vec = lambda part: _mod_spec(d, layer, mod_row, part)
    tab = pl.BlockSpec((tm, 2 * V7X_LANES), lambda bi, i: (i, 0))
    if fft_layout:
        f_spec = pl.BlockSpec((1, FFT_INNER, tm // FFT_INNER, fw), lambda bi, i: (bi, 0, i, 0))
        f_shape = jax.ShapeDtypeStruct((b, FFT_INNER, t // FFT_INNER, fw), F32)
    else:
        f_spec, f_shape = tok(fw), jax.ShapeDtypeStruct((b, t, fw), BF16)
    return pl.pallas_call(
        functools.partial(_in_proj_kernel, widths=widths, row_chunk=min(tm, 4 * BLOCK),
                          fft_layout=fft_layout),
        grid=(b, t // tm),
        in_specs=[tok(d), vec(MOD_SCALE1), vec(MOD_SHIFT1), _resident((1, d)), _resident((d, n_in)),
                  tab, tab, tab,
                  _resident(wsgu.shape), _resident(bsgu.shape), _resident(gsgu.shape),
                  _resident(ones.shape)],
        out_specs=[tok(aw), tok(2 * kvw), pl.BlockSpec((1, n_kv * VT_ROWS, tm), lambda bi, i: (bi, 0, i)),
                   tok(sw), f_spec],
        out_shape=[jax.ShapeDtypeStruct((b, t, aw), BF16),
                   jax.ShapeDtypeStruct((b, t, 2 * kvw), BF16),
                   jax.ShapeDtypeStruct((b, n_kv * VT_ROWS, t), BF16),
                   jax.ShapeDtypeStruct((b, t, sw), F32),
                   f_shape],
        compiler_params=_params("arbitrary", "arbitrary"),
        name="in_proj",
    )(x, mod, mod, g, w_in, *rope_tabs, wsgu, bsgu, gsgu, ones)


def _attn_kernel(sink_ref, q_ref, *refs, local):
    if local:
        kp_ref, ko_ref, kn_ref, vp_ref, vo_ref, vn_ref, ck_ref, cv_ref, o_ref = refs
    else:
        ck_ref, cv_ref, o_ref = refs
    tq = q_ref.shape[1]
    kvw = ck_ref.shape[2] // 2
    n_kv = kvw // HEAD_DIM
    grp = q_ref.shape[2] // HEAD_DIM // n_kv
    cols = grp * BLOCK
    lane = lax.broadcasted_iota(jnp.int32, (BLOCK, V7X_LANES), 1)
    head_lane = lax.broadcasted_iota(jnp.int32, (1, cols), 1)
    if local:
        vt_all = jnp.concatenate([vp_ref[0], vo_ref[0], vn_ref[0]], axis=1)
        key = lax.broadcasted_iota(jnp.int32, (BLOCK, cols), 0)
        qpos = lax.broadcasted_iota(jnp.int32, (BLOCK, cols), 1) & (BLOCK - 1)
        ok_prev = key >= qpos
        ok_next = key <= qpos
        not_first = pl.program_id(1) > 0
        not_last = pl.program_id(1) < pl.num_programs(1) - 1
    colmax = lambda t: jnp.max(t, axis=0, keepdims=True)

    n_sb = tq // BLOCK

    def scores(sb, h):
        qrows = slice(sb * BLOCK, (sb + 1) * BLOCK)
        kcols = slice(h * V7X_LANES, (h + 1) * V7X_LANES)
        parts = []
        for g in range(grp):
            head = h * grp + g
            qc = q_ref[0, qrows, (head // 2) * V7X_LANES:(head // 2 + 1) * V7X_LANES]
            keep = (lane < HEAD_DIM) if head % 2 == 0 else (lane >= HEAD_DIM)
            parts.append(jnp.where(keep, qc, jnp.zeros_like(qc)))
        qs = jnp.concatenate(parts, axis=0)
        tiles = [_dot_nt(ck_ref[0, :, kcols], qs)]
        if local:
            k_win = jnp.concatenate(
                [kp_ref[0, :, kcols], ko_ref[0, :, kcols], kn_ref[0, :, kcols]],
                axis=0)[sb * BLOCK:(sb + 3) * BLOCK]
            s_loc = _dot_nt(k_win, qs)
            s_prev = jnp.where(ok_prev, s_loc[:BLOCK], NEG_INF)
            s_next = jnp.where(ok_next, s_loc[2 * BLOCK:], NEG_INF)
            if sb == 0:
                s_prev = jnp.where(not_first, s_prev, NEG_INF)
            if sb == n_sb - 1:
                s_next = jnp.where(not_last, s_next, NEG_INF)
            tiles += [s_prev, s_loc[BLOCK:2 * BLOCK], s_next]
        return tiles

    def attend(sb, h, tiles):
        vrows = slice(h * VT_ROWS, (h + 1) * VT_ROWS)
        sk = jnp.full((1, cols), sink_ref[h * grp + grp - 1] * LOG2E, F32)
        for g in range(grp - 2, -1, -1):
            sk = jnp.where(head_lane < (g + 1) * BLOCK, sink_ref[h * grp + g] * LOG2E, sk)
        m = sk
        for t in tiles:
            m = jnp.maximum(m, colmax(t))
        acc = _dot(cv_ref[0, vrows, :], jnp.exp2(tiles[0] - m).astype(BF16))
        if local:
            e_loc = jnp.concatenate([jnp.exp2(t - m) for t in tiles[1:]], axis=0)
            acc = acc + _dot(vt_all[vrows, sb * BLOCK:(sb + 3) * BLOCK], e_loc.astype(BF16))
        den = acc[HEAD_DIM:HEAD_DIM + 1, :] + jnp.exp2(sk - m)
        return acc[:HEAD_DIM] / den

    units = [(sb, h) for sb in range(n_sb) for h in range(n_kv)]
    pending = [scores(*un) for un in units[:SCORE_LOOKAHEAD]]
    outs = []
    for u, (sb, h) in enumerate(units):
        tiles = pending.pop(0)
        if u + SCORE_LOOKAHEAD < len(units):
            pending.append(scores(*units[u + SCORE_LOOKAHEAD]))
        outs.append(attend(sb, h, tiles))
        if h == n_kv - 1:
            o_t = jnp.concatenate(outs, axis=0)
            outs = []
            for g in range(grp):
                o_ref[0, sb * BLOCK:(sb + 1) * BLOCK, g * V7X_LANES:(g + 1) * V7X_LANES] = (
                    o_t[:, g * BLOCK:(g + 1) * BLOCK].T.astype(BF16))


def _attention(sink, q, kd, vt, ckd, cvt, tq, local):
    b, t, aw = q.shape
    ctx_len = ckd.shape[1]
    kvw = ckd.shape[2] // 2
    vrows = cvt.shape[1]
    nsb = tq // BLOCK
    nb = t // BLOCK
    smem = pl.BlockSpec(memory_space=pltpu.SMEM)
    qspec = pl.BlockSpec((1, tq, aw), lambda bi, i: (bi, i, 0))
    in_specs = [smem, qspec]
    args = [sink, q]
    if local:
        prev = lambda bi, i: (bi, jnp.maximum(i * nsb - 1, 0))
        nxt = lambda bi, i: (bi, jnp.minimum((i + 1) * nsb, nb - 1))
        in_specs += [pl.BlockSpec((1, BLOCK, 2 * kvw), lambda bi, i: (*prev(bi, i), 0)),
                     pl.BlockSpec((1, tq, 2 * kvw), lambda bi, i: (bi, i, 0)),
                     pl.BlockSpec((1, BLOCK, 2 * kvw), lambda bi, i: (*nxt(bi, i), 0)),
                     pl.BlockSpec((1, vrows, BLOCK), lambda bi, i: (bi, 0, prev(bi, i)[1])),
                     pl.BlockSpec((1, vrows, tq), lambda bi, i: (bi, 0, i)),
                     pl.BlockSpec((1, vrows, BLOCK), lambda bi, i: (bi, 0, nxt(bi, i)[1]))]
        args += [kd, kd, kd, vt, vt, vt]
    in_specs += [pl.BlockSpec((1, ctx_len, 2 * kvw), lambda bi, i: (bi, 0, 0)),
                 pl.BlockSpec((1, vrows, ctx_len), lambda bi, i: (0, 0, bi))]
    args += [ckd, cvt]
    return pl.pallas_call(
        functools.partial(_attn_kernel, local=local),
        grid=(b, t // tq),
        in_specs=in_specs,
        out_specs=qspec,
        out_shape=jax.ShapeDtypeStruct((b, t, aw), BF16),
        compiler_params=_params("arbitrary", "arbitrary"),
        name="attention_local" if local else "attention_ctx",
    )(*args)


def _mxu_const(a):
    return jnp.asarray(a, F32).astype(BF16)


def _dft_tables(n_seq, n2):
    n1 = n_seq // n2
    s2 = np.arange(n2)[:, None, None]
    k1 = np.arange(n1)[None, :, None]
    s1 = np.arange(n1)[None, None, :]
    ang = 2.0 * np.pi * ((k1 * (n2 * s1 + s2)) % n_seq) / n_seq
    m1 = np.concatenate([np.cos(ang), -np.sin(ang)], axis=1)
    a2 = 2.0 * np.pi * ((np.arange(n2)[:, None] * np.arange(n2)[None, :]) % n2) / n2
    c2, s2m = np.cos(a2), np.sin(a2)
    w3 = np.block([[c2, s2m], [-s2m, c2]])
    return _mxu_const(m1), _mxu_const(w3)


def _channel_dft(width):
    d = np.arange(GROUP_DIM)
    a = 2.0 * np.pi * ((d[:, None] * d[None, :]) % GROUP_DIM) / GROUP_DIM
    eye = np.eye(width // GROUP_DIM)
    return np.concatenate([np.kron(eye, np.cos(a)), np.kron(eye, np.sin(a))], axis=0)


def _fft_kernel(x_ref, m_ref, w_ref, cs_ref, o_ref, a_scr, *, n_stage1, pitch, scale):
    i = pl.program_id(1)
    t2, n1, fw = x_ref.shape[1:]
    n2 = w_ref.shape[0] // 2
    t1 = o_ref.shape[2]
    slabs = fw // V7X_LANES

    @pl.when(i < n_stage1)
    def _stage1():
        for j in range(t2):
            a = _dot(m_ref[i * t2 + j], x_ref[0, j].astype(BF16))
            row0 = pl.multiple_of((i * t2 + j) * pitch, V7X_SUBLANES)
            for sl in range(slabs):
                a_scr[sl, pl.ds(row0, 2 * n1), :] = a[:, sl * V7X_LANES:(sl + 1) * V7X_LANES]

    @pl.when(i >= n_stage1)
    def _stage2():
        for t in range(t1):
            k1 = (i - n_stage1) * t1 + t

            def over_s2(row):
                return jnp.concatenate(
                    [a_scr[sl, pl.ds(row, n2, stride=pitch), :] for sl in range(slabs)], axis=1)

            x = jnp.concatenate([over_s2(k1), over_s2(n1 + k1)], axis=0).astype(BF16)
            z = _dot(w_ref[...], x)
            zc = jnp.concatenate([z[:n2], z[n2:]], axis=1).astype(BF16)
            o_ref[0, :, t, :] = _dot(zc, cs_ref[...]) * scale


def _fourier_latent(ft, m1, w3, cs):
    b, n2, n1, fw = ft.shape
    s = n1 * n2
    t2, t1 = 32, 32
    n_stage1, n_stage2 = n2 // t2, n1 // t1
    tile1 = lambda i: jnp.minimum(i, n_stage1 - 1)
    pitch = 2 * n1 + V7X_SUBLANES
    y = pl.pallas_call(
        functools.partial(_fft_kernel, n_stage1=n_stage1, pitch=pitch,
                          scale=1.0 / math.sqrt(s * GROUP_DIM)),
        grid=(b, n_stage1 + n_stage2),
        in_specs=[pl.BlockSpec((1, t2, n1, fw), lambda bi, i: (bi, tile1(i), 0, 0)),
                  _resident(m1.shape), _resident(w3.shape), _resident(cs.shape)],
        out_specs=pl.BlockSpec((1, n2, t1, fw), lambda bi, i: (bi, 0, jnp.maximum(i - n_stage1, 0), 0)),
        out_shape=jax.ShapeDtypeStruct((b, n2, n1, fw), F32),
        scratch_shapes=[pltpu.VMEM((fw // V7X_LANES, n2 * pitch, V7X_LANES), F32)],
        compiler_params=_params("arbitrary", "arbitrary"),
        name="fft",
    )(ft, m1, w3, cs)
    return y.reshape(b, s, fw)


def _fourier_ctx_kernel(f_ref, cs_ref, dl_ref, o_ref, *, scale):
    fw = f_ref.shape[2]
    fc = _dot(f_ref[0], cs_ref[...])
    stacked = jnp.concatenate([fc[:, :fw], fc[:, fw:]], axis=0).astype(BF16)
    o_ref[0] = _dot(dl_ref[...], stacked) * scale


def _fourier_ctx(f, cs_t, dl):
    b, l, fw = f.shape
    return pl.pallas_call(
        functools.partial(_fourier_ctx_kernel, scale=1.0 / math.sqrt(l * GROUP_DIM)),
        grid=(b,),
        in_specs=[pl.BlockSpec((1, l, fw), lambda bi: (bi, 0, 0)),
                  _resident(cs_t.shape), _resident(dl.shape)],
        out_specs=pl.BlockSpec((1, l, fw), lambda bi: (bi, 0, 0)),
        out_shape=jax.ShapeDtypeStruct((b, l, fw), F32),
        compiler_params=_params("arbitrary"),
        name="fourier_ctx",
    )(f, cs_t, dl)


def _out_ff_kernel(x_ref, a_ref, s_ref, f_ref, g1_ref, sc2_ref, sh2_ref, g2_ref,
                   gmix_ref, gpm_ref, gpf_ref, gpo_ref, wo_ref, w1_ref, w2_ref, o_ref, *,
                   ff_chunk, row_chunks):
    aw = a_ref.shape[2]
    sw = s_ref.shape[2]
    gm = gmix_ref[...]
    d_ff = w1_ref.shape[1]
    n_rows = len(row_chunks)
    starts = [sum(row_chunks[:r]) for r in range(n_rows + 1)]
    gate1 = g1_ref[0, 0] * gpm_ref[...]
    gain2 = gpf_ref[...] * (1.0 + sc2_ref[0, 0])
    shift2 = sh2_ref[0, 0]
    gate2 = g2_ref[0, 0] * gpo_ref[...]

    def mix(r):
        rows = slice(starts[r], starts[r + 1])
        ycat = jnp.concatenate([
            (_rms(a_ref[0, rows, :].astype(F32)) * gm[:, :aw]).astype(BF16),
            (_rms(s_ref[0, rows, :]) * gm[:, aw:aw + sw]).astype(BF16),
            (_rms(f_ref[0, rows, :]) * gm[:, aw + sw:]).astype(BF16)], axis=1)
        y = _dot(ycat, wo_ref[...])
        x1 = x_ref[0, rows, :] + _rms(y) * gate1
        return x1, (_rms(x1) * gain2 + shift2).astype(BF16)

    n_ff = d_ff // ff_chunk

    def up(h2, c):
        hid = jnp.maximum(_dot(h2, w1_ref[:, c * ff_chunk:(c + 1) * ff_chunk]), 0.0)
        return (hid * hid).astype(BF16)

    def down(hid, c):
        return _dot(hid, w2_ref[c * ff_chunk:(c + 1) * ff_chunk, :])

    nxt = mix(0)
    for r in range(n_rows):
        x1, h2 = nxt
        hid = up(h2, 0)
        ff = None
        for c in range(n_ff):
            hid_next = up(h2, c + 1) if c + 1 < n_ff else None
            if c == 0 and r + 1 < n_rows:
                nxt = mix(r + 1)
            part = down(hid, c)
            ff = part if ff is None else ff + part
            hid = hid_next
        o_ref[0, starts[r]:starts[r + 1], :] = x1 + _rms(ff) * gate2


def _out_ff(x, a, sg, fo, mod, layer, mod_row, gmix, gpm, gpf, gpo, wo, w1, w2, tm):
    b, t, d = x.shape
    tok = lambda w: pl.BlockSpec((1, tm, w), lambda bi, i: (bi, i, 0))
    vec = lambda part: _mod_spec(d, layer, mod_row, part)
    row_chunks = (min(tm, 4 * BLOCK),) * (tm // min(tm, 4 * BLOCK))
    return pl.pallas_call(
        functools.partial(_out_ff_kernel, ff_chunk=min(w1.shape[1], 1024), row_chunks=row_chunks),
        grid=(b, t // tm),
        in_specs=[tok(d), tok(a.shape[2]), tok(sg.shape[2]), tok(fo.shape[2]),
                  vec(MOD_GATE1), vec(MOD_SCALE2), vec(MOD_SHIFT2), vec(MOD_GATE2),
                  _resident(gmix.shape), _resident(gpm.shape), _resident(gpf.shape), _resident(gpo.shape),
                  _resident(wo.shape), _resident(w1.shape), _resident(w2.shape)],
        out_specs=tok(d),
        out_shape=jax.ShapeDtypeStruct((b, t, d), F32),
        compiler_params=_params("arbitrary", "arbitrary"),
        name="out_ff",
    )(x, a, sg, fo, mod, mod, mod, mod, gmix, gpm, gpf, gpo, wo, w1, w2)


def _rope_tables(seq_len):
    rows = seq_len // GRID_W
    row = jnp.repeat(jnp.arange(rows), GRID_W).astype(F32)
    col = jnp.tile(jnp.arange(GRID_W), rows).astype(F32)
    n_freq = HEAD_DIM // 4
    inv_freq = ROPE_THETA ** (-jnp.arange(n_freq, dtype=F32) / n_freq)
    ang_r = row[:, None] * inv_freq[None, :]
    ang_c = col[:, None] * inv_freq[None, :]
    ang = jnp.concatenate([ang_r, ang_r, ang_c, ang_c], axis=-1)
    reps = V7X_LANES // HEAD_DIM
    cos = jnp.tile(jnp.cos(ang), (1, reps))
    sin = jnp.tile(jnp.sin(ang), (1, reps))
    even = (np.arange(V7X_LANES) // n_freq) % 2 == 0
    return _with_query_scale(cos, jnp.where(even, -sin, 0.0), jnp.where(even, 0.0, sin))


def _with_query_scale(*tables):
    return tuple(jnp.concatenate([t, t * QUERY_SCALE], axis=1) for t in tables)


def kernel(x, c, ctx, c_ctx, w_ada, b_ada, g_pre_mix, w_in, sink, w_sgu, b_sgu, g_sgu,
           g_mix, w_out, g_post_mix, g_pre_ff, w_ff1, w_ff2, g_post_ff):
    b, s, d = x.shape
    ctx_len = ctx.shape[1]
    depth = w_ada.shape[0]
    n_q = sink.shape[1]
    aw = n_q * HEAD_DIM
    n_sgu = w_sgu.shape[1]
    sw = n_sgu * GROUP_DIM
    fw = g_mix.shape[1] - aw - sw
    kvw = (w_in.shape[2] - aw - 2 * sw - fw) // 2
    n_kv = kvw // HEAD_DIM
    grp = n_q // n_kv
    assert n_kv == 2 and kvw == V7X_LANES and s % (8 * BLOCK) == 0 and ctx_len % BLOCK == 0
    assert s % (V7X_SUBLANES * FFT_INNER) == 0 and fw % V7X_LANES == 0
    widths = (aw, kvw, sw, fw)

    rows = -(-(b + 1) // 8) * 8
    cc = jnp.zeros((rows, d), F32).at[:b].set(c).at[b].set(c_ctx)
    mod = _modulation(cc, w_ada, b_ada).reshape(depth, rows, 1, 6 * d)

    rope_x = _rope_tables(s)
    n_ctx = b * ctx_len
    ctx_tile = 8 * BLOCK if n_ctx % (8 * BLOCK) == 0 else ctx_len
    rope_c = _with_query_scale(jnp.ones((n_ctx, V7X_LANES), F32), jnp.zeros((n_ctx, V7X_LANES), F32),
                               jnp.zeros((n_ctx, V7X_LANES), F32))
    ones = _mxu_const(np.kron(np.eye(n_sgu), np.ones((GROUP_DIM, GROUP_DIM))))
    m1, w3 = _dft_tables(s, FFT_INNER)
    cs_np = _channel_dft(fw)
    cs = _mxu_const(cs_np)
    cs_t = _mxu_const(np.concatenate([cs_np[:fw], cs_np[fw:]], axis=1))
    kl = np.arange(ctx_len)
    al = 2.0 * np.pi * ((kl[:, None] * kl[None, :]) % ctx_len) / ctx_len
    dl = _mxu_const(np.concatenate([np.cos(al), -np.sin(al)], axis=1))

    lane = np.arange(aw)
    perm = np.where(lane % V7X_LANES < HEAD_DIM,
                    (lane // V7X_LANES) * HEAD_DIM + lane % V7X_LANES,
                    (grp + lane // V7X_LANES) * HEAD_DIM + lane % V7X_LANES - HEAD_DIM)

    row2 = lambda v: v.reshape(1, -1)
    h_ctx = ctx.reshape(1, n_ctx, d)
    for l in range(depth):
        last = l == depth - 1
        wsgu = w_sgu[l].reshape(n_sgu * CHUNK, CHUNK).astype(BF16)
        bsgu = jnp.repeat(b_sgu[l].T, GROUP_DIM, axis=1)
        gsgu = g_sgu[l].reshape(1, sw)
        gmix = row2(jnp.concatenate([g_mix[l][:aw][perm], g_mix[l][aw:]]))
        wo = jnp.concatenate([w_out[l][:aw][perm], w_out[l][aw:]], axis=0).astype(BF16)
        shared = (row2(g_pre_mix[l]), w_in[l].astype(BF16))
        sgu_w = (wsgu, bsgu, gsgu, ones)
        ff_w = (gmix, row2(g_post_mix[l]), row2(g_pre_ff[l]), row2(g_post_ff[l]), wo,
                w_ff1[l].astype(BF16), w_ff2[l].astype(BF16))

        cq, ckd, cvt, csg, cf = _in_proj(h_ctx, mod, l, b, *shared, rope_c, *sgu_w, ctx_tile, widths,
                                         fft_layout=False)
        ckd = ckd.reshape(b, ctx_len, 2 * kvw)
        if not last:
            ca = _attention(sink[l], cq.reshape(b, ctx_len, aw), None, None, ckd, cvt, ctx_len, local=False)
            cfo = _fourier_ctx(cf.reshape(b, ctx_len, fw), cs_t, dl)
            h_ctx = _out_ff(h_ctx, ca.reshape(1, b * ctx_len, aw), csg, cfo.reshape(1, b * ctx_len, fw),
                            mod, l, b, *ff_w, ctx_tile)

        q, kd, vt, sg, f = _in_proj(x, mod, l, None, *shared, rope_x, *sgu_w, 8 * BLOCK, widths,
                                    fft_layout=True)
        a = _attention(sink[l], q, kd, vt, ckd, cvt, 16 * BLOCK, local=True)
        fo = _fourier_latent(f, m1, w3, cs)
        x = _out_ff(x, a, sg, fo, mod, l, None, *ff_w, 8 * BLOCK)
    return x
```

```python
import functools
import math

import numpy as np
import jax
import jax.numpy as jnp
from jax import lax
from jax.experimental import pallas as pl
from jax.experimental.pallas import tpu as pltpu

F32 = jnp.float32
BF16 = jnp.bfloat16

HEAD_DIM = 64
GRID_W = 64
BLOCK = 128
CHUNK = 128
GROUP_DIM = 64
ROPE_THETA = 10000.0
EPS = 1e-6
NEG_INF = -1e30
LOG2E = math.log2(math.e)
MOD_SHIFT1, MOD_SCALE1, MOD_GATE1, MOD_SHIFT2, MOD_SCALE2, MOD_GATE2 = range(6)
FFT_INNER = 128
SCORE_LOOKAHEAD = 2

V7X_LANES = 128
V7X_SUBLANES = 8
V7X_VMEM_BYTES = 64 * 2**20
VMEM_LIMIT = V7X_VMEM_BYTES * 3 // 4


def _params(*sem):
    return pltpu.CompilerParams(dimension_semantics=sem, vmem_limit_bytes=VMEM_LIMIT)


def _resident(shape):
    return pl.BlockSpec(shape, lambda *_: (0,) * len(shape), pipeline_mode=pl.Buffered(1))


def _rms(x):
    return x * lax.rsqrt(jnp.mean(x * x, axis=-1, keepdims=True) + EPS)


def _gelu_tanh(x):
    c = math.sqrt(2.0 / math.pi)
    return x * (0.5 + 0.5 * jnp.tanh(x * (c + (c * 0.044715) * (x * x))))


def _dot(a, b):
    return jnp.dot(a, b, preferred_element_type=F32)


def _dot_nt(a, b):
    return lax.dot_general(a, b, (((1,), (1,)), ((), ())), preferred_element_type=F32)


def _mod_kernel(cc_ref, w_ref, b_ref, o_ref):
    cc = cc_ref[...]
    s = cc / (1.0 + jnp.exp(-cc))
    o_ref[0] = _dot(s.astype(BF16), w_ref[0].astype(BF16)) + b_ref[0]


def _modulation(cc, w_ada, b_ada):
    depth, d, n = w_ada.shape
    rows = cc.shape[0]
    tn = n // 4
    return pl.pallas_call(
        _mod_kernel,
        grid=(depth, n // tn),
        in_specs=[
            pl.BlockSpec((rows, d), lambda l, j: (0, 0)),
            pl.BlockSpec((1, d, tn), lambda l, j: (l, 0, j)),
            pl.BlockSpec((1, 1, tn), lambda l, j: (l, 0, j)),
        ],
        out_specs=pl.BlockSpec((1, rows, tn), lambda l, j: (l, 0, j)),
        out_shape=jax.ShapeDtypeStruct((depth, rows, n), F32),
        compiler_params=_params("arbitrary", "arbitrary"),
        name="modulation",
    )(cc, w_ada, b_ada.reshape(depth, 1, n))


def _in_proj_kernel(x_ref, sc_ref, sh_ref, g_ref, w_ref, cos_ref, sa_ref, sb_ref,
                    wsgu_ref, bsgu_ref, gsgu_ref, ones_ref,
                    q_ref, kd_ref, v_ref, sgu_ref, f_ref, *, widths, row_chunk, fft_layout):
    aw, kvw, sw, fw = widths
    n_rows = x_ref.shape[1] // row_chunk
    scale = HEAD_DIM ** -0.5 * LOG2E
    n_heads = sw // GROUP_DIM
    lane = lax.broadcasted_iota(jnp.int32, (CHUNK, sw), 1)

    gain = g_ref[...] * (1.0 + sc_ref[0, 0])
    shift = sh_ref[0, 0]

    def modulated(r):
        rows = slice(r * row_chunk, (r + 1) * row_chunk)
        return (_rms(x_ref[0, rows, :]) * gain + shift).astype(BF16)

    def split(r, px):
        base = r * row_chunk
        rows = slice(base, base + row_chunk)
        cos, sa, sb = cos_ref[rows, :], sa_ref[rows, :], sb_ref[rows, :]

        def rope(t):
            return t * cos + pltpu.roll(t, V7X_LANES - 16, 1) * sa + pltpu.roll(t, 16, 1) * sb

        for j in range(aw // V7X_LANES):
            qj = rope(px[:, j * V7X_LANES:(j + 1) * V7X_LANES]) * scale
            q_ref[0, rows, j * V7X_LANES:(j + 1) * V7X_LANES] = qj.astype(BF16)

        k = rope(px[:, aw:aw + kvw])
        kr = pltpu.roll(k, HEAD_DIM, 1)
        lo = lax.broadcasted_iota(jnp.int32, k.shape, 1) < HEAD_DIM
        kd_ref[0, rows, :V7X_LANES] = jnp.where(lo, k, kr).astype(BF16)
        kd_ref[0, rows, V7X_LANES:] = jnp.where(lo, kr, k).astype(BF16)
        vt = px[:, aw + kvw:aw + 2 * kvw].T
        head0 = lax.broadcasted_iota(jnp.int32, vt.shape, 0) < HEAD_DIM
        v_ref[0, :kvw, rows] = jnp.where(head0, vt, 1.0).astype(BF16)
        v_ref[0, kvw:, rows] = jnp.where(head0, 1.0, vt).astype(BF16)

        o = aw + 2 * kvw
        u = _gelu_tanh(px[:, o:o + sw])
        gv = _gelu_tanh(px[:, o + sw:o + 2 * sw])
        ss = gv * gv
        hi = ss.astype(BF16)
        lo2 = (ss - hi.astype(F32)).astype(BF16)
        gsum = _dot(hi, ones_ref[...]) + _dot(lo2, ones_ref[...])
        vn = (gv * lax.rsqrt(gsum * (1.0 / GROUP_DIM) + EPS) * gsgu_ref[...]).astype(BF16)
        for c in range(row_chunk // CHUNK):
            crow = slice(c * CHUNK, (c + 1) * CHUNK)
            m = _dot(wsgu_ref[...], vn[crow])
            mixed = m[(n_heads - 1) * CHUNK:]
            for h in range(n_heads - 2, -1, -1):
                mixed = jnp.where(lane < (h + 1) * GROUP_DIM, m[h * CHUNK:(h + 1) * CHUNK], mixed)
            sgu_ref[0, base + c * CHUNK:base + (c + 1) * CHUNK, :] = u[crow] * (mixed + bsgu_ref[...])

        fcols = px[:, o + 2 * sw:]
        if fft_layout:
            for c in range(row_chunk // FFT_INNER):
                f_ref[0, :, base // FFT_INNER + c, :] = fcols[c * FFT_INNER:(c + 1) * FFT_INNER]
        else:
            f_ref[0, rows, :] = fcols.astype(BF16)

    nxt = _dot(modulated(0), w_ref[...])
    for r in range(n_rows):
        px = nxt
        if r + 1 < n_rows:
            nxt = _dot(modulated(r + 1), w_ref[...])
        split(r, px)


def _mod_spec(d, layer, row, part):
    return pl.BlockSpec((1, 1, 1, d), lambda bi, i: (layer, bi if row is None else row, 0, part))


def _in_proj(x, mod, layer, mod_row, g, w_in, rope_tabs, wsgu, bsgu, gsgu, ones, tm, widths, fft_layout):
    b, t, d = x.shape
    aw, kvw, sw, fw = widths
    n_in = w_in.shape[1]
    tok = lambda w: pl.BlockSpec((1, tm, w), lambda bi, i: (bi, i, 0))
    vec = lambda part: _mod_spec(d, layer, mod_row, part)
    tab = pl.BlockSpec((tm, V7X_LANES), lambda bi, i: (i, 0))
    if fft_layout:
        f_spec = pl.BlockSpec((1, FFT_INNER, tm // FFT_INNER, fw), lambda bi, i: (bi, 0, i, 0))
        f_shape = jax.ShapeDtypeStruct((b, FFT_INNER, t // FFT_INNER, fw), F32)
    else:
        f_spec, f_shape = tok(fw), jax.ShapeDtypeStruct((b, t, fw), BF16)
    return pl.pallas_call(
        functools.partial(_in_proj_kernel, widths=widths, row_chunk=min(tm, 4 * BLOCK),
                          fft_layout=fft_layout),
        grid=(b, t // tm),
        in_specs=[tok(d), vec(MOD_SCALE1), vec(MOD_SHIFT1), _resident((1, d)), _resident((d, n_in)),
                  tab, tab, tab,
                  _resident(wsgu.shape), _resident(bsgu.shape), _resident(gsgu.shape),
                  _resident(ones.shape)],
        out_specs=[tok(aw), tok(2 * kvw), pl.BlockSpec((1, 2 * kvw, tm), lambda bi, i: (bi, 0, i)),
                   tok(sw), f_spec],
        out_shape=[jax.ShapeDtypeStruct((b, t, aw), BF16),
                   jax.ShapeDtypeStruct((b, t, 2 * kvw), BF16),
                   jax.ShapeDtypeStruct((b, 2 * kvw, t), BF16),
                   jax.ShapeDtypeStruct((b, t, sw), F32),
                   f_shape],
        compiler_params=_params("arbitrary", "arbitrary"),
        name="in_proj",
    )(x, mod, mod, g, w_in, *rope_tabs, wsgu, bsgu, gsgu, ones)


def _attn_kernel(sink_ref, q_ref, *refs, local):
    if local:
        kp_ref, ko_ref, kn_ref, vp_ref, vo_ref, vn_ref, ck_ref, cv_ref, o_ref = refs
    else:
        ck_ref, cv_ref, o_ref = refs
    tq = q_ref.shape[1]
    kvw = ck_ref.shape[2] // 2
    n_kv = kvw // HEAD_DIM
    grp = q_ref.shape[2] // HEAD_DIM // n_kv
    cols = grp * BLOCK
    lane = lax.broadcasted_iota(jnp.int32, (BLOCK, V7X_LANES), 1)
    head_lane = lax.broadcasted_iota(jnp.int32, (1, cols), 1)
    kv_row = lax.broadcasted_iota(jnp.int32, (kvw, cols), 0)
    if local:
        vt_all = jnp.concatenate([vp_ref[0], vo_ref[0], vn_ref[0]], axis=1)
        key = lax.broadcasted_iota(jnp.int32, (BLOCK, cols), 0)
        qpos = lax.broadcasted_iota(jnp.int32, (BLOCK, cols), 1) & (BLOCK - 1)
        ok_prev = key >= qpos
        ok_next = key <= qpos
        not_first = pl.program_id(1) > 0
        not_last = pl.program_id(1) < pl.num_programs(1) - 1
    colmax = lambda t: jnp.max(t, axis=0, keepdims=True)

    n_sb = tq // BLOCK

    def scores(sb, h):
        qrows = slice(sb * BLOCK, (sb + 1) * BLOCK)
        kcols = slice(h * V7X_LANES, (h + 1) * V7X_LANES)
        parts = []
        for g in range(grp):
            head = h * grp + g
            qc = q_ref[0, qrows, (head // 2) * V7X_LANES:(head // 2 + 1) * V7X_LANES]
            keep = (lane < HEAD_DIM) if head % 2 == 0 else (lane >= HEAD_DIM)
            parts.append(jnp.where(keep, qc, jnp.zeros_like(qc)))
        qs = jnp.concatenate(parts, axis=0)
        tiles = [_dot_nt(ck_ref[0, :, kcols], qs)]
        if local:
            k_win = jnp.concatenate(
                [kp_ref[0, :, kcols], ko_ref[0, :, kcols], kn_ref[0, :, kcols]],
                axis=0)[sb * BLOCK:(sb + 3) * BLOCK]
            s_loc = _dot_nt(k_win, qs)
            s_prev = jnp.where(ok_prev, s_loc[:BLOCK], NEG_INF)
            s_next = jnp.where(ok_next, s_loc[2 * BLOCK:], NEG_INF)
            if sb == 0:
                s_prev = jnp.where(not_first, s_prev, NEG_INF)
            if sb == n_sb - 1:
                s_next = jnp.where(not_last, s_next, NEG_INF)
            tiles += [s_prev, s_loc[BLOCK:2 * BLOCK], s_next]
        return tiles

    def attend(sb, h, tiles):
        vrows = slice(h * kvw, (h + 1) * kvw)
        ones_row = ((h + 1) % n_kv) * HEAD_DIM
        sk = jnp.full((1, cols), sink_ref[h * grp + grp - 1] * LOG2E, F32)
        for g in range(grp - 2, -1, -1):
            sk = jnp.where(head_lane < (g + 1) * BLOCK, sink_ref[h * grp + g] * LOG2E, sk)
        m = sk
        for t in tiles:
            m = jnp.maximum(m, colmax(t))
        acc = _dot(cv_ref[0, vrows, :], jnp.exp2(tiles[0] - m).astype(BF16))
        if local:
            e_loc = jnp.concatenate([jnp.exp2(t - m) for t in tiles[1:]], axis=0)
            acc = acc + _dot(vt_all[vrows, sb * BLOCK:(sb + 3) * BLOCK], e_loc.astype(BF16))
        den = acc[ones_row:ones_row + 1, :] + jnp.exp2(sk - m)
        return acc / den

    units = [(sb, h) for sb in range(n_sb) for h in range(n_kv)]
    pending = [scores(*un) for un in units[:SCORE_LOOKAHEAD]]
    outs = []
    for u, (sb, h) in enumerate(units):
        tiles = pending.pop(0)
        if u + SCORE_LOOKAHEAD < len(units):
            pending.append(scores(*units[u + SCORE_LOOKAHEAD]))
        outs.append(attend(sb, h, tiles))
        if h == n_kv - 1:
            o_t = jnp.where(kv_row < HEAD_DIM, outs[0], outs[1])
            outs = []
            for g in range(grp):
                o_ref[0, sb * BLOCK:(sb + 1) * BLOCK, g * V7X_LANES:(g + 1) * V7X_LANES] = (
                    o_t[:, g * BLOCK:(g + 1) * BLOCK].T.astype(BF16))


def _attention(sink, q, kd, vt, ckd, cvt, tq, local):
    b, t, aw = q.shape
    ctx_len = ckd.shape[1]
    kvw = cvt.shape[1] // 2
    nsb = tq // BLOCK
    nb = t // BLOCK
    smem = pl.BlockSpec(memory_space=pltpu.SMEM)
    qspec = pl.BlockSpec((1, tq, aw), lambda bi, i: (bi, i, 0))
    in_specs = [smem, qspec]
    args = [sink, q]
    if local:
        prev = lambda bi, i: (bi, jnp.maximum(i * nsb - 1, 0))
        nxt = lambda bi, i: (bi, jnp.minimum((i + 1) * nsb, nb - 1))
        in_specs += [pl.BlockSpec((1, BLOCK, 2 * kvw), lambda bi, i: (*prev(bi, i), 0)),
                     pl.BlockSpec((1, tq, 2 * kvw), lambda bi, i: (bi, i, 0)),
                     pl.BlockSpec((1, BLOCK, 2 * kvw), lambda bi, i: (*nxt(bi, i), 0)),
                     pl.BlockSpec((1, 2 * kvw, BLOCK), lambda bi, i: (bi, 0, prev(bi, i)[1])),
                     pl.BlockSpec((1, 2 * kvw, tq), lambda bi, i: (bi, 0, i)),
                     pl.BlockSpec((1, 2 * kvw, BLOCK), lambda bi, i: (bi, 0, nxt(bi, i)[1]))]
        args += [kd, kd, kd, vt, vt, vt]
    in_specs += [pl.BlockSpec((1, ctx_len, 2 * kvw), lambda bi, i: (bi, 0, 0)),
                 pl.BlockSpec((1, 2 * kvw, ctx_len), lambda bi, i: (0, 0, bi))]
    args += [ckd, cvt]
    return pl.pallas_call(
        functools.partial(_attn_kernel, local=local),
        grid=(b, t // tq),
        in_specs=in_specs,
        out_specs=qspec,
        out_shape=jax.ShapeDtypeStruct((b, t, aw), BF16),
        compiler_params=_params("arbitrary", "arbitrary"),
        name="attention_local" if local else "attention_ctx",
    )(*args)


def _mxu_const(a):
    return jnp.asarray(a, F32).astype(BF16)


def _dft_tables(n_seq, n2):
    n1 = n_seq // n2
    s2 = np.arange(n2)[:, None, None]
    k1 = np.arange(n1)[None, :, None]
    s1 = np.arange(n1)[None, None, :]
    ang = 2.0 * np.pi * ((k1 * (n2 * s1 + s2)) % n_seq) / n_seq
    m1 = np.concatenate([np.cos(ang), -np.sin(ang)], axis=1)
    a2 = 2.0 * np.pi * ((np.arange(n2)[:, None] * np.arange(n2)[None, :]) % n2) / n2
    c2, s2m = np.cos(a2), np.sin(a2)
    w3 = np.block([[c2, s2m], [-s2m, c2]])
    return _mxu_const(m1), _mxu_const(w3)


def _channel_dft(width):
    d = np.arange(GROUP_DIM)
    a = 2.0 * np.pi * ((d[:, None] * d[None, :]) % GROUP_DIM) / GROUP_DIM
    eye = np.eye(width // GROUP_DIM)
    return np.concatenate([np.kron(eye, np.cos(a)), np.kron(eye, np.sin(a))], axis=0)


def _fft_kernel(x_ref, m_ref, w_ref, cs_ref, o_ref, a_scr, *, n_stage1, pitch, scale):
    i = pl.program_id(1)
    t2, n1, fw = x_ref.shape[1:]
    n2 = w_ref.shape[0] // 2
    t1 = o_ref.shape[2]
    slabs = fw // V7X_LANES

    @pl.when(i < n_stage1)
    def _stage1():
        for j in range(t2):
            a = _dot(m_ref[j], x_ref[0, j].astype(BF16))
            row0 = pl.multiple_of((i * t2 + j) * pitch, V7X_SUBLANES)
            for sl in range(slabs):
                a_scr[sl, pl.ds(row0, 2 * n1), :] = a[:, sl * V7X_LANES:(sl + 1) * V7X_LANES]

    @pl.when(i >= n_stage1)
    def _stage2():
        for t in range(t1):
            k1 = (i - n_stage1) * t1 + t

            def over_s2(row):
                return jnp.concatenate(
                    [a_scr[sl, pl.ds(row, n2, stride=pitch), :] for sl in range(slabs)], axis=1)

            x = jnp.concatenate([over_s2(k1), over_s2(n1 + k1)], axis=0).astype(BF16)
            z = _dot(w_ref[...], x)
            zc = jnp.concatenate([z[:n2], z[n2:]], axis=1).astype(BF16)
            o_ref[0, :, t, :] = _dot(zc, cs_ref[...]) * scale


def _fourier_latent(ft, m1, w3, cs):
    b, n2, n1, fw = ft.shape
    s = n1 * n2
    t2, t1 = 32, 32
    n_stage1, n_stage2 = n2 // t2, n1 // t1
    tile1 = lambda i: jnp.minimum(i, n_stage1 - 1)
    pitch = 2 * n1 + V7X_SUBLANES
    y = pl.pallas_call(
        functools.partial(_fft_kernel, n_stage1=n_stage1, pitch=pitch,
                          scale=1.0 / math.sqrt(s * GROUP_DIM)),
        grid=(b, n_stage1 + n_stage2),
        in_specs=[pl.BlockSpec((1, t2, n1, fw), lambda bi, i: (bi, tile1(i), 0, 0)),
                  pl.BlockSpec((t2, 2 * n1, n1), lambda bi, i: (tile1(i), 0, 0)),
                  _resident(w3.shape), _resident(cs.shape)],
        out_specs=pl.BlockSpec((1, n2, t1, fw), lambda bi, i: (bi, 0, jnp.maximum(i - n_stage1, 0), 0)),
        out_shape=jax.ShapeDtypeStruct((b, n2, n1, fw), F32),
        scratch_shapes=[pltpu.VMEM((fw // V7X_LANES, n2 * pitch, V7X_LANES), F32)],
        compiler_params=_params("arbitrary", "arbitrary"),
        name="fft",
    )(ft, m1, w3, cs)
    return y.reshape(b, s, fw)


def _fourier_ctx_kernel(f_ref, cs_ref, dl_ref, o_ref, *, scale):
    fw = f_ref.shape[2]
    fc = _dot(f_ref[0], cs_ref[...])
    stacked = jnp.concatenate([fc[:, :fw], fc[:, fw:]], axis=0).astype(BF16)
    o_ref[0] = _dot(dl_ref[...], stacked) * scale


def _fourier_ctx(f, cs_t, dl):
    b, l, fw = f.shape
    return pl.pallas_call(
        functools.partial(_fourier_ctx_kernel, scale=1.0 / math.sqrt(l * GROUP_DIM)),
        grid=(b,),
        in_specs=[pl.BlockSpec((1, l, fw), lambda bi: (bi, 0, 0)),
                  _resident(cs_t.shape), _resident(dl.shape)],
        out_specs=pl.BlockSpec((1, l, fw), lambda bi: (bi, 0, 0)),
        out_shape=jax.ShapeDtypeStruct((b, l, fw), F32),
        compiler_params=_params("arbitrary"),
        name="fourier_ctx",
    )(f, cs_t, dl)


def _out_ff_kernel(x_ref, a_ref, s_ref, f_ref, g1_ref, sc2_ref, sh2_ref, g2_ref,
                   gmix_ref, gpm_ref, gpf_ref, gpo_ref, wo_ref, w1_ref, w2_ref, o_ref, *,
                   ff_chunk, row_chunks):
    aw = a_ref.shape[2]
    sw = s_ref.shape[2]
    gm = gmix_ref[...]
    d_ff = w1_ref.shape[1]
    n_rows = len(row_chunks)
    starts = [sum(row_chunks[:r]) for r in range(n_rows + 1)]
    gate1 = g1_ref[0, 0] * gpm_ref[...]
    gain2 = gpf_ref[...] * (1.0 + sc2_ref[0, 0])
    shift2 = sh2_ref[0, 0]
    gate2 = g2_ref[0, 0] * gpo_ref[...]

    def mix(r):
        rows = slice(starts[r], starts[r + 1])
        ycat = jnp.concatenate([
            (_rms(a_ref[0, rows, :].astype(F32)) * gm[:, :aw]).astype(BF16),
            (_rms(s_ref[0, rows, :]) * gm[:, aw:aw + sw]).astype(BF16),
            (_rms(f_ref[0, rows, :]) * gm[:, aw + sw:]).astype(BF16)], axis=1)
        y = _dot(ycat, wo_ref[...])
        x1 = x_ref[0, rows, :] + _rms(y) * gate1
        return x1, (_rms(x1) * gain2 + shift2).astype(BF16)

    n_ff = d_ff // ff_chunk

    def up(h2, c):
        hid = jnp.maximum(_dot(h2, w1_ref[:, c * ff_chunk:(c + 1) * ff_chunk]), 0.0)
        return (hid * hid).astype(BF16)

    def down(hid, c):
        return _dot(hid, w2_ref[c * ff_chunk:(c + 1) * ff_chunk, :])

    nxt = mix(0)
    for r in range(n_rows):
        x1, h2 = nxt
        hid = up(h2, 0)
        ff = None
        for c in range(n_ff):
            hid_next = up(h2, c + 1) if c + 1 < n_ff else None
            if c == 0 and r + 1 < n_rows:
                nxt = mix(r + 1)
            part = down(hid, c)
            ff = part if ff is None else ff + part
            hid = hid_next
        o_ref[0, starts[r]:starts[r + 1], :] = x1 + _rms(ff) * gate2


def _out_ff(x, a, sg, fo, mod, layer, mod_row, gmix, gpm, gpf, gpo, wo, w1, w2, tm):
    b, t, d = x.shape
    tok = lambda w: pl.BlockSpec((1, tm, w), lambda bi, i: (bi, i, 0))
    vec = lambda part: _mod_spec(d, layer, mod_row, part)
    row_chunks = (min(tm, 4 * BLOCK),) * (tm // min(tm, 4 * BLOCK))
    return pl.pallas_call(
        functools.partial(_out_ff_kernel, ff_chunk=min(w1.shape[1], 1024), row_chunks=row_chunks),
        grid=(b, t // tm),
        in_specs=[tok(d), tok(a.shape[2]), tok(sg.shape[2]), tok(fo.shape[2]),
                  vec(MOD_GATE1), vec(MOD_SCALE2), vec(MOD_SHIFT2), vec(MOD_GATE2),
                  _resident(gmix.shape), _resident(gpm.shape), _resident(gpf.shape), _resident(gpo.shape),
                  _resident(wo.shape), _resident(w1.shape), _resident(w2.shape)],
        out_specs=tok(d),
        out_shape=jax.ShapeDtypeStruct((b, t, d), F32),
        compiler_params=_params("arbitrary", "arbitrary"),
        name="out_ff",
    )(x, a, sg, fo, mod, mod, mod, mod, gmix, gpm, gpf, gpo, wo, w1, w2)


def _rope_tables(seq_len):
    rows = seq_len // GRID_W
    row = jnp.repeat(jnp.arange(rows), GRID_W).astype(F32)
    col = jnp.tile(jnp.arange(GRID_W), rows).astype(F32)
    n_freq = HEAD_DIM // 4
    inv_freq = ROPE_THETA ** (-jnp.arange(n_freq, dtype=F32) / n_freq)
    ang_r = row[:, None] * inv_freq[None, :]
    ang_c = col[:, None] * inv_freq[None, :]
    ang = jnp.concatenate([ang_r, ang_r, ang_c, ang_c], axis=-1)
    reps = V7X_LANES // HEAD_DIM
    cos = jnp.tile(jnp.cos(ang), (1, reps))
    sin = jnp.tile(jnp.sin(ang), (1, reps))
    even = (np.arange(V7X_LANES) // n_freq) % 2 == 0
    return cos, jnp.where(even, -sin, 0.0), jnp.where(even, 0.0, sin)


def kernel(x, c, ctx, c_ctx, w_ada, b_ada, g_pre_mix, w_in, sink, w_sgu, b_sgu, g_sgu,
           g_mix, w_out, g_post_mix, g_pre_ff, w_ff1, w_ff2, g_post_ff):
    b, s, d = x.shape
    ctx_len = ctx.shape[1]
    depth = w_ada.shape[0]
    n_q = sink.shape[1]
    aw = n_q * HEAD_DIM
    n_sgu = w_sgu.shape[1]
    sw = n_sgu * GROUP_DIM
    fw = g_mix.shape[1] - aw - sw
    kvw = (w_in.shape[2] - aw - 2 * sw - fw) // 2
    n_kv = kvw // HEAD_DIM
    grp = n_q // n_kv
    assert n_kv == 2 and kvw == V7X_LANES and s % (8 * BLOCK) == 0 and ctx_len % BLOCK == 0
    assert s % (V7X_SUBLANES * FFT_INNER) == 0 and fw % V7X_LANES == 0
    widths = (aw, kvw, sw, fw)

    rows = -(-(b + 1) // 8) * 8
    cc = jnp.zeros((rows, d), F32).at[:b].set(c).at[b].set(c_ctx)
    mod = _modulation(cc, w_ada, b_ada).reshape(depth, rows, 1, 6 * d)

    rope_x = _rope_tables(s)
    n_ctx = b * ctx_len
    ctx_tile = 8 * BLOCK if n_ctx % (8 * BLOCK) == 0 else ctx_len
    rope_c = (jnp.ones((n_ctx, V7X_LANES), F32), jnp.zeros((n_ctx, V7X_LANES), F32),
              jnp.zeros((n_ctx, V7X_LANES), F32))
    ones = _mxu_const(np.kron(np.eye(n_sgu), np.ones((GROUP_DIM, GROUP_DIM))))
    m1, w3 = _dft_tables(s, FFT_INNER)
    cs_np = _channel_dft(fw)
    cs = _mxu_const(cs_np)
    cs_t = _mxu_const(np.concatenate([cs_np[:fw], cs_np[fw:]], axis=1))
    kl = np.arange(ctx_len)
    al = 2.0 * np.pi * ((kl[:, None] * kl[None, :]) % ctx_len) / ctx_len
    dl = _mxu_const(np.concatenate([np.cos(al), -np.sin(al)], axis=1))

    lane = np.arange(aw)
    perm = np.where(lane % V7X_LANES < HEAD_DIM,
                    (lane // V7X_LANES) * HEAD_DIM + lane % V7X_LANES,
                    (grp + lane // V7X_LANES) * HEAD_DIM + lane % V7X_LANES - HEAD_DIM)

    row2 = lambda v: v.reshape(1, -1)
    h_ctx = ctx.reshape(1, n_ctx, d)
    for l in range(depth):
        last = l == depth - 1
        wsgu = w_sgu[l].reshape(n_sgu * CHUNK, CHUNK).astype(BF16)
        bsgu = jnp.repeat(b_sgu[l].T, GROUP_DIM, axis=1)
        gsgu = g_sgu[l].reshape(1, sw)
        gmix = row2(jnp.concatenate([g_mix[l][:aw][perm], g_mix[l][aw:]]))
        wo = jnp.concatenate([w_out[l][:aw][perm], w_out[l][aw:]], axis=0).astype(BF16)
        shared = (row2(g_pre_mix[l]), w_in[l].astype(BF16))
        sgu_w = (wsgu, bsgu, gsgu, ones)
        ff_w = (gmix, row2(g_post_mix[l]), row2(g_pre_ff[l]), row2(g_post_ff[l]), wo,
                w_ff1[l].astype(BF16), w_ff2[l].astype(BF16))

        cq, ckd, cvt, csg, cf = _in_proj(h_ctx, mod, l, b, *shared, rope_c, *sgu_w, ctx_tile, widths,
                                         fft_layout=False)
        ckd = ckd.reshape(b, ctx_len, 2 * kvw)
        if not last:
            ca = _attention(sink[l], cq.reshape(b, ctx_len, aw), None, None, ckd, cvt, ctx_len, local=False)
            cfo = _fourier_ctx(cf.reshape(b, ctx_len, fw), cs_t, dl)
            h_ctx = _out_ff(h_ctx, ca.reshape(1, b * ctx_len, aw), csg, cfo.reshape(1, b * ctx_len, fw),
                            mod, l, b, *ff_w, ctx_tile)

        q, kd, vt, sg, f = _in_proj(x, mod, l, None, *shared, rope_x, *sgu_w, 16 * BLOCK, widths,
                                    fft_layout=True)
        a = _attention(sink[l], q, kd, vt, ckd, cvt, 16 * BLOCK, local=True)
        fo = _fourier_latent(f, m1, w3, cs)
        x = _out_ff(x, a, sg, fo, mod, l, None, *ff_w, 8 * BLOCK)
    return x
```

```python
import functools
import math

import numpy as np
import jax
import jax.numpy as jnp
from jax import lax
from jax.experimental import pallas as pl
from jax.experimental.pallas import tpu as pltpu

F32 = jnp.float32
BF16 = jnp.bfloat16

HEAD_DIM = 64
GRID_W = 64
BLOCK = 128
CHUNK = 128
GROUP_DIM = 64
ROPE_THETA = 10000.0
EPS = 1e-6
NEG_INF = -1e30
LOG2E = math.log2(math.e)
MOD_SHIFT1, MOD_SCALE1, MOD_GATE1, MOD_SHIFT2, MOD_SCALE2, MOD_GATE2 = range(6)
FFT_INNER = 128
SCORE_LOOKAHEAD = 4

V7X_LANES = 128
V7X_SUBLANES = 8
V7X_VMEM_BYTES = 64 * 2**20
VMEM_LIMIT = V7X_VMEM_BYTES * 3 // 4


def _params(*sem):
    return pltpu.CompilerParams(dimension_semantics=sem, vmem_limit_bytes=VMEM_LIMIT)


def _resident(shape):
    return pl.BlockSpec(shape, lambda *_: (0,) * len(shape), pipeline_mode=pl.Buffered(1))


def _rms(x):
    return x * lax.rsqrt(jnp.mean(x * x, axis=-1, keepdims=True) + EPS)


def _gelu_tanh(x):
    c = math.sqrt(2.0 / math.pi)
    return x * (0.5 + 0.5 * jnp.tanh(x * (c + (c * 0.044715) * (x * x))))


def _dot(a, b):
    return jnp.dot(a, b, preferred_element_type=F32)


def _dot_nt(a, b):
    return lax.dot_general(a, b, (((1,), (1,)), ((), ())), preferred_element_type=F32)


def _mod_kernel(cc_ref, w_ref, b_ref, o_ref):
    cc = cc_ref[...]
    s = cc / (1.0 + jnp.exp(-cc))
    o_ref[0] = _dot(s.astype(BF16), w_ref[0].astype(BF16)) + b_ref[0]


def _modulation(cc, w_ada, b_ada):
    depth, d, n = w_ada.shape
    rows = cc.shape[0]
    tn = n // 4
    return pl.pallas_call(
        _mod_kernel,
        grid=(depth, n // tn),
        in_specs=[
            pl.BlockSpec((rows, d), lambda l, j: (0, 0)),
            pl.BlockSpec((1, d, tn), lambda l, j: (l, 0, j)),
            pl.BlockSpec((1, 1, tn), lambda l, j: (l, 0, j)),
        ],
        out_specs=pl.BlockSpec((1, rows, tn), lambda l, j: (l, 0, j)),
        out_shape=jax.ShapeDtypeStruct((depth, rows, n), F32),
        compiler_params=_params("arbitrary", "arbitrary"),
        name="modulation",
    )(cc, w_ada, b_ada.reshape(depth, 1, n))


def _in_proj_kernel(x_ref, sc_ref, sh_ref, g_ref, w_ref, cos_ref, sa_ref, sb_ref,
                    wsgu_ref, bsgu_ref, gsgu_ref, ones_ref,
                    q_ref, kd_ref, v_ref, sgu_ref, f_ref, *, widths, row_chunk, fft_layout):
    aw, kvw, sw, fw = widths
    n_rows = x_ref.shape[1] // row_chunk
    scale = HEAD_DIM ** -0.5 * LOG2E
    n_heads = sw // GROUP_DIM
    lane = lax.broadcasted_iota(jnp.int32, (CHUNK, sw), 1)

    gain = g_ref[...] * (1.0 + sc_ref[0, 0])
    shift = sh_ref[0, 0]

    def modulated(r):
        rows = slice(r * row_chunk, (r + 1) * row_chunk)
        return (_rms(x_ref[0, rows, :]) * gain + shift).astype(BF16)

    def split(r, px):
        base = r * row_chunk
        rows = slice(base, base + row_chunk)
        cos, sa, sb = cos_ref[rows, :], sa_ref[rows, :], sb_ref[rows, :]

        def rope(t):
            return t * cos + pltpu.roll(t, V7X_LANES - 16, 1) * sa + pltpu.roll(t, 16, 1) * sb

        for j in range(aw // V7X_LANES):
            qj = rope(px[:, j * V7X_LANES:(j + 1) * V7X_LANES]) * scale
            q_ref[0, rows, j * V7X_LANES:(j + 1) * V7X_LANES] = qj.astype(BF16)

        k = rope(px[:, aw:aw + kvw])
        kr = pltpu.roll(k, HEAD_DIM, 1)
        lo = lax.broadcasted_iota(jnp.int32, k.shape, 1) < HEAD_DIM
        kd_ref[0, rows, :V7X_LANES] = jnp.where(lo, k, kr).astype(BF16)
        kd_ref[0, rows, V7X_LANES:] = jnp.where(lo, kr, k).astype(BF16)
        vt = px[:, aw + kvw:aw + 2 * kvw].T
        head0 = lax.broadcasted_iota(jnp.int32, vt.shape, 0) < HEAD_DIM
        v_ref[0, :kvw, rows] = jnp.where(head0, vt, 1.0).astype(BF16)
        v_ref[0, kvw:, rows] = jnp.where(head0, 1.0, vt).astype(BF16)

        o = aw + 2 * kvw
        u = _gelu_tanh(px[:, o:o + sw])
        gv = _gelu_tanh(px[:, o + sw:o + 2 * sw])
        ss = gv * gv
        hi = ss.astype(BF16)
        lo2 = (ss - hi.astype(F32)).astype(BF16)
        gsum = _dot(hi, ones_ref[...]) + _dot(lo2, ones_ref[...])
        vn = (gv * lax.rsqrt(gsum * (1.0 / GROUP_DIM) + EPS) * gsgu_ref[...]).astype(BF16)
        for c in range(row_chunk // CHUNK):
            crow = slice(c * CHUNK, (c + 1) * CHUNK)
            m = _dot(wsgu_ref[...], vn[crow])
            mixed = m[(n_heads - 1) * CHUNK:]
            for h in range(n_heads - 2, -1, -1):
                mixed = jnp.where(lane < (h + 1) * GROUP_DIM, m[h * CHUNK:(h + 1) * CHUNK], mixed)
            sgu_ref[0, base + c * CHUNK:base + (c + 1) * CHUNK, :] = u[crow] * (mixed + bsgu_ref[...])

        fcols = px[:, o + 2 * sw:]
        if fft_layout:
            for c in range(row_chunk // FFT_INNER):
                f_ref[0, :, base // FFT_INNER + c, :] = fcols[c * FFT_INNER:(c + 1) * FFT_INNER]
        else:
            f_ref[0, rows, :] = fcols.astype(BF16)

    nxt = _dot(modulated(0), w_ref[...])
    for r in range(n_rows):
        px = nxt
        if r + 1 < n_rows:
            nxt = _dot(modulated(r + 1), w_ref[...])
        split(r, px)


def _mod_spec(d, layer, row, part):
    return pl.BlockSpec((1, 1, 1, d), lambda bi, i: (layer, bi if row is None else row, 0, part))


def _in_proj(x, mod, layer, mod_row, g, w_in, rope_tabs, wsgu, bsgu, gsgu, ones, tm, widths, fft_layout):
    b, t, d = x.shape
    aw, kvw, sw, fw = widths
    n_in = w_in.shape[1]
    tok = lambda w: pl.BlockSpec((1, tm, w), lambda bi, i: (bi, i, 0))
    vec = lambda part: _mod_spec(d, layer, mod_row, part)
    tab = pl.BlockSpec((tm, V7X_LANES), lambda bi, i: (i, 0))
    if fft_layout:
        f_spec = pl.BlockSpec((1, FFT_INNER, tm // FFT_INNER, fw), lambda bi, i: (bi, 0, i, 0))
        f_shape = jax.ShapeDtypeStruct((b, FFT_INNER, t // FFT_INNER, fw), F32)
    else:
        f_spec, f_shape = tok(fw), jax.ShapeDtypeStruct((b, t, fw), BF16)
    return pl.pallas_call(
        functools.partial(_in_proj_kernel, widths=widths, row_chunk=min(tm, 4 * BLOCK),
                          fft_layout=fft_layout),
        grid=(b, t // tm),
        in_specs=[tok(d), vec(MOD_SCALE1), vec(MOD_SHIFT1), _resident((1, d)), _resident((d, n_in)),
                  tab, tab, tab,
                  _resident(wsgu.shape), _resident(bsgu.shape), _resident(gsgu.shape),
                  _resident(ones.shape)],
        out_specs=[tok(aw), tok(2 * kvw), pl.BlockSpec((1, 2 * kvw, tm), lambda bi, i: (bi, 0, i)),
                   tok(sw), f_spec],
        out_shape=[jax.ShapeDtypeStruct((b, t, aw), BF16),
                   jax.ShapeDtypeStruct((b, t, 2 * kvw), BF16),
                   jax.ShapeDtypeStruct((b, 2 * kvw, t), BF16),
                   jax.ShapeDtypeStruct((b, t, sw), F32),
                   f_shape],
        compiler_params=_params("arbitrary", "arbitrary"),
        name="in_proj",
    )(x, mod, mod, g, w_in, *rope_tabs, wsgu, bsgu, gsgu, ones)


def _attn_kernel(sink_ref, q_ref, *refs, local):
    if local:
        kp_ref, ko_ref, kn_ref, vp_ref, vo_ref, vn_ref, ck_ref, cv_ref, o_ref = refs
    else:
        ck_ref, cv_ref, o_ref = refs
    tq = q_ref.shape[1]
    kvw = ck_ref.shape[2] // 2
    n_kv = kvw // HEAD_DIM
    grp = q_ref.shape[2] // HEAD_DIM // n_kv
    half = BLOCK // 2
    cols = grp * half
    n_local = 2 * BLOCK + half
    lane = lax.broadcasted_iota(jnp.int32, (half, V7X_LANES), 1)
    head_lane = lax.broadcasted_iota(jnp.int32, (1, cols), 1)
    kv_row = lax.broadcasted_iota(jnp.int32, (kvw, cols), 0)
    if local:
        vt_all = jnp.concatenate([vp_ref[0], vo_ref[0], vn_ref[0]], axis=1)
        key = lax.broadcasted_iota(jnp.int32, (half, cols), 0)
        qoff = lax.broadcasted_iota(jnp.int32, (half, cols), 1) & (half - 1)
        ok_lo = key >= qoff
        ok_hi = key <= qoff
        not_first = pl.program_id(1) > 0
        not_last = pl.program_id(1) < pl.num_programs(1) - 1
    colmax = lambda t: jnp.max(t, axis=0, keepdims=True)

    n_sb = tq // BLOCK
    n_tiles = n_local // half

    def scores(sb, hf, h):
        qrows = slice(sb * BLOCK + hf * half, sb * BLOCK + (hf + 1) * half)
        kcols = slice(h * V7X_LANES, (h + 1) * V7X_LANES)
        parts = []
        for g in range(grp):
            head = h * grp + g
            qc = q_ref[0, qrows, (head // 2) * V7X_LANES:(head // 2 + 1) * V7X_LANES]
            keep = (lane < HEAD_DIM) if head % 2 == 0 else (lane >= HEAD_DIM)
            parts.append(jnp.where(keep, qc, jnp.zeros_like(qc)))
        qs = jnp.concatenate(parts, axis=0)
        tiles = [_dot_nt(ck_ref[0, :, kcols], qs)]
        if local:
            start = sb * BLOCK + hf * half
            k_loc = jnp.concatenate(
                [kp_ref[0, :, kcols], ko_ref[0, :, kcols], kn_ref[0, :, kcols]],
                axis=0)[start:start + n_local]
            s_loc = _dot_nt(k_loc, qs)
            t = [s_loc[j * half:(j + 1) * half] for j in range(n_tiles)]
            t[0] = jnp.where(ok_lo, t[0], NEG_INF)
            t[-1] = jnp.where(ok_hi, t[-1], NEG_INF)
            if sb == 0:
                for j in range(2 - hf):
                    t[j] = jnp.where(not_first, t[j], NEG_INF)
            if sb == n_sb - 1:
                for j in range(n_tiles - 1 - hf, n_tiles):
                    t[j] = jnp.where(not_last, t[j], NEG_INF)
            tiles += t
        return tiles

    def attend(sb, hf, h, tiles):
        vrows = slice(h * kvw, (h + 1) * kvw)
        ones_row = ((h + 1) % n_kv) * HEAD_DIM
        sk = jnp.full((1, cols), sink_ref[h * grp + grp - 1] * LOG2E, F32)
        for g in range(grp - 2, -1, -1):
            sk = jnp.where(head_lane < (g + 1) * half, sink_ref[h * grp + g] * LOG2E, sk)
        m = sk
        for t in tiles:
            m = jnp.maximum(m, colmax(t))
        acc = _dot(cv_ref[0, vrows, :], jnp.exp2(tiles[0] - m).astype(BF16))
        if local:
            start = sb * BLOCK + hf * half
            e_loc = jnp.concatenate([jnp.exp2(t - m) for t in tiles[1:]], axis=0)
            acc = acc + _dot(vt_all[vrows, start:start + n_local], e_loc.astype(BF16))
        den = acc[ones_row:ones_row + 1, :] + jnp.exp2(sk - m)
        return acc / den

    units = [(sb, hf, h) for sb in range(n_sb) for hf in range(2) for h in range(n_kv)]
    pending = [scores(*un) for un in units[:SCORE_LOOKAHEAD]]
    outs = []
    for u, (sb, hf, h) in enumerate(units):
        tiles = pending.pop(0)
        if u + SCORE_LOOKAHEAD < len(units):
            pending.append(scores(*units[u + SCORE_LOOKAHEAD]))
        outs.append(attend(sb, hf, h, tiles))
        if hf == 1 and h == n_kv - 1:
            o_hf = [jnp.where(kv_row < HEAD_DIM, outs[2 * k], outs[2 * k + 1]) for k in range(2)]
            outs = []
            for g in range(grp):
                o_g = jnp.concatenate([o[:, g * half:(g + 1) * half] for o in o_hf], axis=1)
                o_ref[0, sb * BLOCK:(sb + 1) * BLOCK, g * V7X_LANES:(g + 1) * V7X_LANES] = (
                    o_g.T.astype(BF16))


def _attention(sink, q, kd, vt, ckd, cvt, tq, local):
    b, t, aw = q.shape
    ctx_len = ckd.shape[1]
    kvw = cvt.shape[1] // 2
    nsb = tq // BLOCK
    nb = t // BLOCK
    smem = pl.BlockSpec(memory_space=pltpu.SMEM)
    qspec = pl.BlockSpec((1, tq, aw), lambda bi, i: (bi, i, 0))
    in_specs = [smem, qspec]
    args = [sink, q]
    if local:
        prev = lambda bi, i: (bi, jnp.maximum(i * nsb - 1, 0))
        nxt = lambda bi, i: (bi, jnp.minimum((i + 1) * nsb, nb - 1))
        in_specs += [pl.BlockSpec((1, BLOCK, 2 * kvw), lambda bi, i: (*prev(bi, i), 0)),
                     pl.BlockSpec((1, tq, 2 * kvw), lambda bi, i: (bi, i, 0)),
                     pl.BlockSpec((1, BLOCK, 2 * kvw), lambda bi, i: (*nxt(bi, i), 0)),
                     pl.BlockSpec((1, 2 * kvw, BLOCK), lambda bi, i: (bi, 0, prev(bi, i)[1])),
                     pl.BlockSpec((1, 2 * kvw, tq), lambda bi, i: (bi, 0, i)),
                     pl.BlockSpec((1, 2 * kvw, BLOCK), lambda bi, i: (bi, 0, nxt(bi, i)[1]))]
        args += [kd, kd, kd, vt, vt, vt]
    in_specs += [pl.BlockSpec((1, ctx_len, 2 * kvw), lambda bi, i: (bi, 0, 0)),
                 pl.BlockSpec((1, 2 * kvw, ctx_len), lambda bi, i: (0, 0, bi))]
    args += [ckd, cvt]
    return pl.pallas_call(
        functools.partial(_attn_kernel, local=local),
        grid=(b, t // tq),
        in_specs=in_specs,
        out_specs=qspec,
        out_shape=jax.ShapeDtypeStruct((b, t, aw), BF16),
        compiler_params=_params("arbitrary", "arbitrary"),
        name="attention_local" if local else "attention_ctx",
    )(*args)


def _mxu_const(a):
    return jnp.asarray(a, F32).astype(BF16)


def _dft_tables(n_seq, n2):
    n1 = n_seq // n2
    s2 = np.arange(n2)[:, None, None]
    k1 = np.arange(n1)[None, :, None]
    s1 = np.arange(n1)[None, None, :]
    ang = 2.0 * np.pi * ((k1 * (n2 * s1 + s2)) % n_seq) / n_seq
    m1 = np.concatenate([np.cos(ang), -np.sin(ang)], axis=1)
    a2 = 2.0 * np.pi * ((np.arange(n2)[:, None] * np.arange(n2)[None, :]) % n2) / n2
    c2, s2m = np.cos(a2), np.sin(a2)
    w3 = np.block([[c2, s2m], [-s2m, c2]])
    return _mxu_const(m1), _mxu_const(w3)


def _channel_dft(width):
    d = np.arange(GROUP_DIM)
    a = 2.0 * np.pi * ((d[:, None] * d[None, :]) % GROUP_DIM) / GROUP_DIM
    eye = np.eye(width // GROUP_DIM)
    return np.concatenate([np.kron(eye, np.cos(a)), np.kron(eye, np.sin(a))], axis=0)


def _fft_kernel(x_ref, m_ref, w_ref, cs_ref, o_ref, a_scr, *, n_stage1, pitch, scale):
    i = pl.program_id(1)
    t2, n1, fw = x_ref.shape[1:]
    n2 = w_ref.shape[0] // 2
    t1 = o_ref.shape[2]
    slabs = fw // V7X_LANES

    @pl.when(i < n_stage1)
    def _stage1():
        for j in range(t2):
            a = _dot(m_ref[j], x_ref[0, j].astype(BF16))
            row0 = pl.multiple_of((i * t2 + j) * pitch, V7X_SUBLANES)
            for sl in range(slabs):
                a_scr[sl, pl.ds(row0, 2 * n1), :] = a[:, sl * V7X_LANES:(sl + 1) * V7X_LANES]

    @pl.when(i >= n_stage1)
    def _stage2():
        for t in range(t1):
            k1 = (i - n_stage1) * t1 + t

            def over_s2(row):
                return jnp.concatenate(
                    [a_scr[sl, pl.ds(row, n2, stride=pitch), :] for sl in range(slabs)], axis=1)

            x = jnp.concatenate([over_s2(k1), over_s2(n1 + k1)], axis=0).astype(BF16)
            z = _dot(w_ref[...], x)
            zc = jnp.concatenate([z[:n2], z[n2:]], axis=1).astype(BF16)
            o_ref[0, :, t, :] = _dot(zc, cs_ref[...]) * scale


def _fourier_latent(ft, m1, w3, cs):
    b, n2, n1, fw = ft.shape
    s = n1 * n2
    t2, t1 = 32, 32
    n_stage1, n_stage2 = n2 // t2, n1 // t1
    tile1 = lambda i: jnp.minimum(i, n_stage1 - 1)
    pitch = 2 * n1 + V7X_SUBLANES
    y = pl.pallas_call(
        functools.partial(_fft_kernel, n_stage1=n_stage1, pitch=pitch,
                          scale=1.0 / math.sqrt(s * GROUP_DIM)),
        grid=(b, n_stage1 + n_stage2),
        in_specs=[pl.BlockSpec((1, t2, n1, fw), lambda bi, i: (bi, tile1(i), 0, 0)),
                  pl.BlockSpec((t2, 2 * n1, n1), lambda bi, i: (tile1(i), 0, 0)),
                  _resident(w3.shape), _resident(cs.shape)],
        out_specs=pl.BlockSpec((1, n2, t1, fw), lambda bi, i: (bi, 0, jnp.maximum(i - n_stage1, 0), 0)),
        out_shape=jax.ShapeDtypeStruct((b, n2, n1, fw), F32),
        scratch_shapes=[pltpu.VMEM((fw // V7X_LANES, n2 * pitch, V7X_LANES), F32)],
        compiler_params=_params("arbitrary", "arbitrary"),
        name="fft",
    )(ft, m1, w3, cs)
    return y.reshape(b, s, fw)


def _fourier_ctx_kernel(f_ref, cs_ref, dl_ref, o_ref, *, scale):
    fw = f_ref.shape[2]
    fc = _dot(f_ref[0], cs_ref[...])
    stacked = jnp.concatenate([fc[:, :fw], fc[:, fw:]], axis=0).astype(BF16)
    o_ref[0] = _dot(dl_ref[...], stacked) * scale


def _fourier_ctx(f, cs_t, dl):
    b, l, fw = f.shape
    return pl.pallas_call(
        functools.partial(_fourier_ctx_kernel, scale=1.0 / math.sqrt(l * GROUP_DIM)),
        grid=(b,),
        in_specs=[pl.BlockSpec((1, l, fw), lambda bi: (bi, 0, 0)),
                  _resident(cs_t.shape), _resident(dl.shape)],
        out_specs=pl.BlockSpec((1, l, fw), lambda bi: (bi, 0, 0)),
        out_shape=jax.ShapeDtypeStruct((b, l, fw), F32),
        compiler_params=_params("arbitrary"),
        name="fourier_ctx",
    )(f, cs_t, dl)


def _out_ff_kernel(x_ref, a_ref, s_ref, f_ref, g1_ref, sc2_ref, sh2_ref, g2_ref,
                   gmix_ref, gpm_ref, gpf_ref, gpo_ref, wo_ref, w1_ref, w2_ref, o_ref, *,
                   ff_chunk, row_chunks):
    aw = a_ref.shape[2]
    sw = s_ref.shape[2]
    gm = gmix_ref[...]
    d_ff = w1_ref.shape[1]
    n_rows = len(row_chunks)
    starts = [sum(row_chunks[:r]) for r in range(n_rows + 1)]
    gate1 = g1_ref[0, 0] * gpm_ref[...]
    gain2 = gpf_ref[...] * (1.0 + sc2_ref[0, 0])
    shift2 = sh2_ref[0, 0]
    gate2 = g2_ref[0, 0] * gpo_ref[...]

    def mix(r):
        rows = slice(starts[r], starts[r + 1])
        ycat = jnp.concatenate([
            (_rms(a_ref[0, rows, :].astype(F32)) * gm[:, :aw]).astype(BF16),
            (_rms(s_ref[0, rows, :]) * gm[:, aw:aw + sw]).astype(BF16),
            (_rms(f_ref[0, rows, :]) * gm[:, aw + sw:]).astype(BF16)], axis=1)
        y = _dot(ycat, wo_ref[...])
        x1 = x_ref[0, rows, :] + _rms(y) * gate1
        return x1, (_rms(x1) * gain2 + shift2).astype(BF16)

    n_ff = d_ff // ff_chunk

    def up(h2, c):
        hid = jnp.maximum(_dot(h2, w1_ref[:, c * ff_chunk:(c + 1) * ff_chunk]), 0.0)
        return (hid * hid).astype(BF16)

    def down(hid, c):
        return _dot(hid, w2_ref[c * ff_chunk:(c + 1) * ff_chunk, :])

    nxt = mix(0)
    for r in range(n_rows):
        x1, h2 = nxt
        hid = up(h2, 0)
        ff = None
        for c in range(n_ff):
            hid_next = up(h2, c + 1) if c + 1 < n_ff else None
            if c == 0 and r + 1 < n_rows:
                nxt = mix(r + 1)
            part = down(hid, c)
            ff = part if ff is None else ff + part
            hid = hid_next
        o_ref[0, starts[r]:starts[r + 1], :] = x1 + _rms(ff) * gate2


def _out_ff(x, a, sg, fo, mod, layer, mod_row, gmix, gpm, gpf, gpo, wo, w1, w2, tm):
    b, t, d = x.shape
    tok = lambda w: pl.BlockSpec((1, tm, w), lambda bi, i: (bi, i, 0))
    vec = lambda part: _mod_spec(d, layer, mod_row, part)
    row_chunks = (min(tm, 4 * BLOCK),) * (tm // min(tm, 4 * BLOCK))
    return pl.pallas_call(
        functools.partial(_out_ff_kernel, ff_chunk=min(w1.shape[1], 1024), row_chunks=row_chunks),
        grid=(b, t // tm),
        in_specs=[tok(d), tok(a.shape[2]), tok(sg.shape[2]), tok(fo.shape[2]),
                  vec(MOD_GATE1), vec(MOD_SCALE2), vec(MOD_SHIFT2), vec(MOD_GATE2),
                  _resident(gmix.shape), _resident(gpm.shape), _resident(gpf.shape), _resident(gpo.shape),
                  _resident(wo.shape), _resident(w1.shape), _resident(w2.shape)],
        out_specs=tok(d),
        out_shape=jax.ShapeDtypeStruct((b, t, d), F32),
        compiler_params=_params("arbitrary", "arbitrary"),
        name="out_ff",
    )(x, a, sg, fo, mod, mod, mod, mod, gmix, gpm, gpf, gpo, wo, w1, w2)


def _rope_tables(seq_len):
    rows = seq_len // GRID_W
    row = jnp.repeat(jnp.arange(rows), GRID_W).astype(F32)
    col = jnp.tile(jnp.arange(GRID_W), rows).astype(F32)
    n_freq = HEAD_DIM // 4
    inv_freq = ROPE_THETA ** (-jnp.arange(n_freq, dtype=F32) / n_freq)
    ang_r = row[:, None] * inv_freq[None, :]
    ang_c = col[:, None] * inv_freq[None, :]
    ang = jnp.concatenate([ang_r, ang_r, ang_c, ang_c], axis=-1)
    reps = V7X_LANES // HEAD_DIM
    cos = jnp.tile(jnp.cos(ang), (1, reps))
    sin = jnp.tile(jnp.sin(ang), (1, reps))
    even = (np.arange(V7X_LANES) // n_freq) % 2 == 0
    return cos, jnp.where(even, -sin, 0.0), jnp.where(even, 0.0, sin)


def kernel(x, c, ctx, c_ctx, w_ada, b_ada, g_pre_mix, w_in, sink, w_sgu, b_sgu, g_sgu,
           g_mix, w_out, g_post_mix, g_pre_ff, w_ff1, w_ff2, g_post_ff):
    b, s, d = x.shape
    ctx_len = ctx.shape[1]
    depth = w_ada.shape[0]
    n_q = sink.shape[1]
    aw = n_q * HEAD_DIM
    n_sgu = w_sgu.shape[1]
    sw = n_sgu * GROUP_DIM
    fw = g_mix.shape[1] - aw - sw
    kvw = (w_in.shape[2] - aw - 2 * sw - fw) // 2
    n_kv = kvw // HEAD_DIM
    grp = n_q // n_kv
    assert n_kv == 2 and kvw == V7X_LANES and s % (8 * BLOCK) == 0 and ctx_len % BLOCK == 0
    assert s % (V7X_SUBLANES * FFT_INNER) == 0 and fw % V7X_LANES == 0
    widths = (aw, kvw, sw, fw)

    rows = -(-(b + 1) // 8) * 8
    cc = jnp.zeros((rows, d), F32).at[:b].set(c).at[b].set(c_ctx)
    mod = _modulation(cc, w_ada, b_ada).reshape(depth, rows, 1, 6 * d)

    rope_x = _rope_tables(s)
    n_ctx = b * ctx_len
    ctx_tile = 8 * BLOCK if n_ctx % (8 * BLOCK) == 0 else ctx_len
    rope_c = (jnp.ones((n_ctx, V7X_LANES), F32), jnp.zeros((n_ctx, V7X_LANES), F32),
              jnp.zeros((n_ctx, V7X_LANES), F32))
    ones = _mxu_const(np.kron(np.eye(n_sgu), np.ones((GROUP_DIM, GROUP_DIM))))
    m1, w3 = _dft_tables(s, FFT_INNER)
    cs_np = _channel_dft(fw)
    cs = _mxu_const(cs_np)
    cs_t = _mxu_const(np.concatenate([cs_np[:fw], cs_np[fw:]], axis=1))
    kl = np.arange(ctx_len)
    al = 2.0 * np.pi * ((kl[:, None] * kl[None, :]) % ctx_len) / ctx_len
    dl = _mxu_const(np.concatenate([np.cos(al), -np.sin(al)], axis=1))

    lane = np.arange(aw)
    perm = np.where(lane % V7X_LANES < HEAD_DIM,
                    (lane // V7X_LANES) * HEAD_DIM + lane % V7X_LANES,
                    (grp + lane // V7X_LANES) * HEAD_DIM + lane % V7X_LANES - HEAD_DIM)

    row2 = lambda v: v.reshape(1, -1)
    h_ctx = ctx.reshape(1, n_ctx, d)
    for l in range(depth):
        last = l == depth - 1
        wsgu = w_sgu[l].reshape(n_sgu * CHUNK, CHUNK).astype(BF16)
        bsgu = jnp.repeat(b_sgu[l].T, GROUP_DIM, axis=1)
        gsgu = g_sgu[l].reshape(1, sw)
        gmix = row2(jnp.concatenate([g_mix[l][:aw][perm], g_mix[l][aw:]]))
        wo = jnp.concatenate([w_out[l][:aw][perm], w_out[l][aw:]], axis=0).astype(BF16)
        shared = (row2(g_pre_mix[l]), w_in[l].astype(BF16))
        sgu_w = (wsgu, bsgu, gsgu, ones)
        ff_w = (gmix, row2(g_post_mix[l]), row2(g_pre_ff[l]), row2(g_post_ff[l]), wo,
                w_ff1[l].astype(BF16), w_ff2[l].astype(BF16))

        cq, ckd, cvt, csg, cf = _in_proj(h_ctx, mod, l, b, *shared, rope_c, *sgu_w, ctx_tile, widths,
                                         fft_layout=False)
        ckd = ckd.reshape(b, ctx_len, 2 * kvw)
        if not last:
            ca = _attention(sink[l], cq.reshape(b, ctx_len, aw), None, None, ckd, cvt, ctx_len, local=False)
            cfo = _fourier_ctx(cf.reshape(b, ctx_len, fw), cs_t, dl)
            h_ctx = _out_ff(h_ctx, ca.reshape(1, b * ctx_len, aw), csg, cfo.reshape(1, b * ctx_len, fw),
                            mod, l, b, *ff_w, ctx_tile)

        q, kd, vt, sg, f = _in_proj(x, mod, l, None, *shared, rope_x, *sgu_w, 8 * BLOCK, widths,
                                    fft_layout=True)
        a = _attention(sink[l], q, kd, vt, ckd, cvt, 16 * BLOCK, local=True)
        fo = _fourier_latent(f, m1, w3, cs)
        x = _out_ff(x, a, sg, fo, mod, l, None, *ff_w, 8 * BLOCK)
    return x
```

```python
import functools
import math

import numpy as np
import jax
import jax.numpy as jnp
from jax import lax
from jax.experimental import pallas as pl
from jax.experimental.pallas import tpu as pltpu

F32 = jnp.float32
BF16 = jnp.bfloat16

HEAD_DIM = 64
GRID_W = 64
BLOCK = 128
CHUNK = 128
GROUP_DIM = 64
ROPE_THETA = 10000.0
EPS = 1e-6
NEG_INF = -1e30
LOG2E = math.log2(math.e)
MOD_SHIFT1, MOD_SCALE1, MOD_GATE1, MOD_SHIFT2, MOD_SCALE2, MOD_GATE2 = range(6)
FFT_INNER = 128
SCORE_LOOKAHEAD = 4

V7X_LANES = 128
V7X_SUBLANES = 8
V7X_VMEM_BYTES = 64 * 2**20
VMEM_LIMIT = V7X_VMEM_BYTES * 3 // 4


def _params(*sem):
    return pltpu.CompilerParams(dimension_semantics=sem, vmem_limit_bytes=VMEM_LIMIT)


def _resident(shape):
    return pl.BlockSpec(shape, lambda *_: (0,) * len(shape), pipeline_mode=pl.Buffered(1))


def _rms(x):
    return x * lax.rsqrt(jnp.mean(x * x, axis=-1, keepdims=True) + EPS)


def _gelu_tanh(x):
    c = math.sqrt(2.0 / math.pi)
    return x * (0.5 + 0.5 * jnp.tanh(x * (c + (c * 0.044715) * (x * x))))


def _dot(a, b):
    return jnp.dot(a, b, preferred_element_type=F32)


def _dot_nt(a, b):
    return lax.dot_general(a, b, (((1,), (1,)), ((), ())), preferred_element_type=F32)


def _mod_kernel(cc_ref, w_ref, b_ref, o_ref):
    cc = cc_ref[...]
    s = cc / (1.0 + jnp.exp(-cc))
    o_ref[0] = _dot(s.astype(BF16), w_ref[0].astype(BF16)) + b_ref[0]


def _modulation(cc, w_ada, b_ada):
    depth, d, n = w_ada.shape
    rows = cc.shape[0]
    tn = n // 4
    return pl.pallas_call(
        _mod_kernel,
        grid=(depth, n // tn),
        in_specs=[
            pl.BlockSpec((rows, d), lambda l, j: (0, 0)),
            pl.BlockSpec((1, d, tn), lambda l, j: (l, 0, j)),
            pl.BlockSpec((1, 1, tn), lambda l, j: (l, 0, j)),
        ],
        out_specs=pl.BlockSpec((1, rows, tn), lambda l, j: (l, 0, j)),
        out_shape=jax.ShapeDtypeStruct((depth, rows, n), F32),
        compiler_params=_params("arbitrary", "arbitrary"),
        name="modulation",
    )(cc, w_ada, b_ada.reshape(depth, 1, n))


def _in_proj_kernel(x_ref, sc_ref, sh_ref, g_ref, w_ref, cos_ref, sa_ref, sb_ref,
                    wsgu_ref, bsgu_ref, gsgu_ref, ones_ref,
                    q_ref, kd_ref, v_ref, sgu_ref, f_ref, *, widths, row_chunk, fft_layout):
    aw, kvw, sw, fw = widths
    n_rows = x_ref.shape[1] // row_chunk
    scale = HEAD_DIM ** -0.5 * LOG2E
    n_heads = sw // GROUP_DIM
    lane = lax.broadcasted_iota(jnp.int32, (CHUNK, sw), 1)

    gain = g_ref[...] * (1.0 + sc_ref[0, 0])
    shift = sh_ref[0, 0]

    def modulated(r):
        rows = slice(r * row_chunk, (r + 1) * row_chunk)
        return (_rms(x_ref[0, rows, :]) * gain + shift).astype(BF16)

    def split(r, px):
        base = r * row_chunk
        rows = slice(base, base + row_chunk)
        cos, sa, sb = cos_ref[rows, :], sa_ref[rows, :], sb_ref[rows, :]

        def rope(t):
            return t * cos + pltpu.roll(t, V7X_LANES - 16, 1) * sa + pltpu.roll(t, 16, 1) * sb

        for j in range(aw // V7X_LANES):
            qj = rope(px[:, j * V7X_LANES:(j + 1) * V7X_LANES]) * scale
            q_ref[0, rows, j * V7X_LANES:(j + 1) * V7X_LANES] = qj.astype(BF16)

        k = rope(px[:, aw:aw + kvw])
        kr = pltpu.roll(k, HEAD_DIM, 1)
        lo = lax.broadcasted_iota(jnp.int32, k.shape, 1) < HEAD_DIM
        kd_ref[0, rows, :V7X_LANES] = jnp.where(lo, k, kr).astype(BF16)
        kd_ref[0, rows, V7X_LANES:] = jnp.where(lo, kr, k).astype(BF16)
        vt = px[:, aw + kvw:aw + 2 * kvw].T
        head0 = lax.broadcasted_iota(jnp.int32, vt.shape, 0) < HEAD_DIM
        v_ref[0, :kvw, rows] = jnp.where(head0, vt, 1.0).astype(BF16)
        v_ref[0, kvw:, rows] = jnp.where(head0, 1.0, vt).astype(BF16)

        o = aw + 2 * kvw
        u = _gelu_tanh(px[:, o:o + sw])
        gv = _gelu_tanh(px[:, o + sw:o + 2 * sw])
        ss = gv * gv
        hi = ss.astype(BF16)
        lo2 = (ss - hi.astype(F32)).astype(BF16)
        gsum = _dot(hi, ones_ref[...]) + _dot(lo2, ones_ref[...])
        vn = (gv * lax.rsqrt(gsum * (1.0 / GROUP_DIM) + EPS) * gsgu_ref[...]).astype(BF16)
        for c in range(row_chunk // CHUNK):
            crow = slice(c * CHUNK, (c + 1) * CHUNK)
            m = _dot(wsgu_ref[...], vn[crow])
            mixed = m[(n_heads - 1) * CHUNK:]
            for h in range(n_heads - 2, -1, -1):
                mixed = jnp.where(lane < (h + 1) * GROUP_DIM, m[h * CHUNK:(h + 1) * CHUNK], mixed)
            sgu_ref[0, base + c * CHUNK:base + (c + 1) * CHUNK, :] = u[crow] * (mixed + bsgu_ref[...])

        fcols = px[:, o + 2 * sw:]
        if fft_layout:
            for c in range(row_chunk // FFT_INNER):
                f_ref[0, :, base // FFT_INNER + c, :] = fcols[c * FFT_INNER:(c + 1) * FFT_INNER]
        else:
            f_ref[0, rows, :] = fcols.astype(BF16)

    nxt = _dot(modulated(0), w_ref[...])
    for r in range(n_rows):
        px = nxt
        if r + 1 < n_rows:
            nxt = _dot(modulated(r + 1), w_ref[...])
        split(r, px)


def _mod_spec(d, layer, row, part):
    return pl.BlockSpec((1, 1, 1, d), lambda bi, i: (layer, bi if row is None else row, 0, part))


def _in_proj(x, mod, layer, mod_row, g, w_in, rope_tabs, wsgu, bsgu, gsgu, ones, tm, widths, fft_layout):
    b, t, d = x.shape
    aw, kvw, sw, fw = widths
    n_in = w_in.shape[1]
    tok = lambda w: pl.BlockSpec((1, tm, w), lambda bi, i: (bi, i, 0))
    vec = lambda part: _mod_spec(d, layer, mod_row, part)
    tab = pl.BlockSpec((tm, V7X_LANES), lambda bi, i: (i, 0))
    if fft_layout:
        f_spec = pl.BlockSpec((1, FFT_INNER, tm // FFT_INNER, fw), lambda bi, i: (bi, 0, i, 0))
        f_shape = jax.ShapeDtypeStruct((b, FFT_INNER, t // FFT_INNER, fw), F32)
    else:
        f_spec, f_shape = tok(fw), jax.ShapeDtypeStruct((b, t, fw), BF16)
    return pl.pallas_call(
        functools.partial(_in_proj_kernel, widths=widths, row_chunk=min(tm, 4 * BLOCK),
                          fft_layout=fft_layout),
        grid=(b, t // tm),
        in_specs=[tok(d), vec(MOD_SCALE1), vec(MOD_SHIFT1), _resident((1, d)), _resident((d, n_in)),
                  tab, tab, tab,
                  _resident(wsgu.shape), _resident(bsgu.shape), _resident(gsgu.shape),
                  _resident(ones.shape)],
        out_specs=[tok(aw), tok(2 * kvw), pl.BlockSpec((1, 2 * kvw, tm), lambda bi, i: (bi, 0, i)),
                   tok(sw), f_spec],
        out_shape=[jax.ShapeDtypeStruct((b, t, aw), BF16),
                   jax.ShapeDtypeStruct((b, t, 2 * kvw), BF16),
                   jax.ShapeDtypeStruct((b, 2 * kvw, t), BF16),
                   jax.ShapeDtypeStruct((b, t, sw), F32),
                   f_shape],
        compiler_params=_params("arbitrary", "arbitrary"),
        name="in_proj",
    )(x, mod, mod, g, w_in, *rope_tabs, wsgu, bsgu, gsgu, ones)


def _attn_kernel(sink_ref, q_ref, *refs, local):
    if local:
        kp_ref, ko_ref, kn_ref, vp_ref, vo_ref, vn_ref, ck_ref, cv_ref, o_ref = refs
    else:
        ck_ref, cv_ref, o_ref = refs
    tq = q_ref.shape[1]
    kvw = ck_ref.shape[2] // 2
    n_kv = kvw // HEAD_DIM
    grp = q_ref.shape[2] // HEAD_DIM // n_kv
    half = BLOCK // 2
    cols = grp * half
    n_local = 2 * BLOCK + half
    lane = lax.broadcasted_iota(jnp.int32, (half, V7X_LANES), 1)
    head_lane = lax.broadcasted_iota(jnp.int32, (1, cols), 1)
    kv_row = lax.broadcasted_iota(jnp.int32, (kvw, cols), 0)
    if local:
        vt_all = jnp.concatenate([vp_ref[0], vo_ref[0], vn_ref[0]], axis=1)
        key = lax.broadcasted_iota(jnp.int32, (half, cols), 0)
        qoff = lax.broadcasted_iota(jnp.int32, (half, cols), 1) & (half - 1)
        ok_lo = key >= qoff
        ok_hi = key <= qoff
        not_first = pl.program_id(1) > 0
        not_last = pl.program_id(1) < pl.num_programs(1) - 1
    colmax = lambda t: jnp.max(t, axis=0, keepdims=True)

    n_sb = tq // BLOCK
    n_tiles = n_local // half

    def scores(sb, hf, h):
        qrows = slice(sb * BLOCK + hf * half, sb * BLOCK + (hf + 1) * half)
        kcols = slice(h * V7X_LANES, (h + 1) * V7X_LANES)
        parts = []
        for g in range(grp):
            head = h * grp + g
            qc = q_ref[0, qrows, (head // 2) * V7X_LANES:(head // 2 + 1) * V7X_LANES]
            keep = (lane < HEAD_DIM) if head % 2 == 0 else (lane >= HEAD_DIM)
            parts.append(jnp.where(keep, qc, jnp.zeros_like(qc)))
        qs = jnp.concatenate(parts, axis=0)
        tiles = [_dot_nt(ck_ref[0, :, kcols], qs)]
        if local:
            start = sb * BLOCK + hf * half
            k_loc = jnp.concatenate(
                [kp_ref[0, :, kcols], ko_ref[0, :, kcols], kn_ref[0, :, kcols]],
                axis=0)[start:start + n_local]
            s_loc = _dot_nt(k_loc, qs)
            t = [s_loc[j * half:(j + 1) * half] for j in range(n_tiles)]
            t[0] = jnp.where(ok_lo, t[0], NEG_INF)
            t[-1] = jnp.where(ok_hi, t[-1], NEG_INF)
            if sb == 0:
                for j in range(2 - hf):
                    t[j] = jnp.where(not_first, t[j], NEG_INF)
            if sb == n_sb - 1:
                for j in range(n_tiles - 1 - hf, n_tiles):
                    t[j] = jnp.where(not_last, t[j], NEG_INF)
            tiles += t
        return tiles

    def attend(sb, hf, h, tiles):
        vrows = slice(h * kvw, (h + 1) * kvw)
        ones_row = ((h + 1) % n_kv) * HEAD_DIM
        sk = jnp.full((1, cols), sink_ref[h * grp + grp - 1] * LOG2E, F32)
        for g in range(grp - 2, -1, -1):
            sk = jnp.where(head_lane < (g + 1) * half, sink_ref[h * grp + g] * LOG2E, sk)
        m = sk
        for t in tiles:
            m = jnp.maximum(m, colmax(t))
        acc = _dot(cv_ref[0, vrows, :], jnp.exp2(tiles[0] - m).astype(BF16))
        if local:
            start = sb * BLOCK + hf * half
            e_loc = jnp.concatenate([jnp.exp2(t - m) for t in tiles[1:]], axis=0)
            acc = acc + _dot(vt_all[vrows, start:start + n_local], e_loc.astype(BF16))
        den = acc[ones_row:ones_row + 1, :] + jnp.exp2(sk - m)
        return acc / den

    units = [(sb, hf, h) for sb in range(n_sb) for hf in range(2) for h in range(n_kv)]
    pending = [scores(*un) for un in units[:SCORE_LOOKAHEAD]]
    outs = []
    for u, (sb, hf, h) in enumerate(units):
        tiles = pending.pop(0)
        if u + SCORE_LOOKAHEAD < len(units):
            pending.append(scores(*units[u + SCORE_LOOKAHEAD]))
        outs.append(attend(sb, hf, h, tiles))
        if hf == 1 and h == n_kv - 1:
            o_hf = [jnp.where(kv_row < HEAD_DIM, outs[2 * k], outs[2 * k + 1]) for k in range(2)]
            outs = []
            for g in range(grp):
                o_g = jnp.concatenate([o[:, g * half:(g + 1) * half] for o in o_hf], axis=1)
                o_ref[0, sb * BLOCK:(sb + 1) * BLOCK, g * V7X_LANES:(g + 1) * V7X_LANES] = (
                    o_g.T.astype(BF16))


def _attention(sink, q, kd, vt, ckd, cvt, tq, local):
    b, t, aw = q.shape
    ctx_len = ckd.shape[1]
    kvw = cvt.shape[1] // 2
    nsb = tq // BLOCK
    nb = t // BLOCK
    smem = pl.BlockSpec(memory_space=pltpu.SMEM)
    qspec = pl.BlockSpec((1, tq, aw), lambda bi, i: (bi, i, 0))
    in_specs = [smem, qspec]
    args = [sink, q]
    if local:
        prev = lambda bi, i: (bi, jnp.maximum(i * nsb - 1, 0))
        nxt = lambda bi, i: (bi, jnp.minimum((i + 1) * nsb, nb - 1))
        in_specs += [pl.BlockSpec((1, BLOCK, 2 * kvw), lambda bi, i: (*prev(bi, i), 0)),
                     pl.BlockSpec((1, tq, 2 * kvw), lambda bi, i: (bi, i, 0)),
                     pl.BlockSpec((1, BLOCK, 2 * kvw), lambda bi, i: (*nxt(bi, i), 0)),
                     pl.BlockSpec((1, 2 * kvw, BLOCK), lambda bi, i: (bi, 0, prev(bi, i)[1])),
                     pl.BlockSpec((1, 2 * kvw, tq), lambda bi, i: (bi, 0, i)),
                     pl.BlockSpec((1, 2 * kvw, BLOCK), lambda bi, i: (bi, 0, nxt(bi, i)[1]))]
        args += [kd, kd, kd, vt, vt, vt]
    in_specs += [pl.BlockSpec((1, ctx_len, 2 * kvw), lambda bi, i: (bi, 0, 0)),
                 pl.BlockSpec((1, 2 * kvw, ctx_len), lambda bi, i: (0, 0, bi))]
    args += [ckd, cvt]
    return pl.pallas_call(
        functools.partial(_attn_kernel, local=local),
        grid=(b, t // tq),
        in_specs=in_specs,
        out_specs=qspec,
        out_shape=jax.ShapeDtypeStruct((b, t, aw), BF16),
        compiler_params=_params("arbitrary", "arbitrary"),
        name="attention_local" if local else "attention_ctx",
    )(*args)


def _mxu_const(a):
    return jnp.asarray(a, F32).astype(BF16)


def _dft_tables(n_seq, n2):
    n1 = n_seq // n2
    s2 = np.arange(n2)[:, None, None]
    k1 = np.arange(n1)[None, :, None]
    s1 = np.arange(n1)[None, None, :]
    ang = 2.0 * np.pi * ((k1 * (n2 * s1 + s2)) % n_seq) / n_seq
    m1 = np.concatenate([np.cos(ang), -np.sin(ang)], axis=1)
    a2 = 2.0 * np.pi * ((np.arange(n2)[:, None] * np.arange(n2)[None, :]) % n2) / n2
    c2, s2m = np.cos(a2), np.sin(a2)
    w3 = np.block([[c2, s2m], [-s2m, c2]])
    return _mxu_const(m1), _mxu_const(w3)


def _channel_dft(width):
    d = np.arange(GROUP_DIM)
    a = 2.0 * np.pi * ((d[:, None] * d[None, :]) % GROUP_DIM) / GROUP_DIM
    eye = np.eye(width // GROUP_DIM)
    return np.concatenate([np.kron(eye, np.cos(a)), np.kron(eye, np.sin(a))], axis=0)


def _fft_kernel(x_ref, m_ref, w_ref, cs_ref, o_ref, a_scr, *, n_stage1, pitch, scale):
    i = pl.program_id(1)
    t2, n1, fw = x_ref.shape[1:]
    n2 = w_ref.shape[0] // 2
    t1 = o_ref.shape[2]
    slabs = fw // V7X_LANES

    @pl.when(i < n_stage1)
    def _stage1():
        for j in range(t2):
            a = _dot(m_ref[j], x_ref[0, j].astype(BF16))
            row0 = pl.multiple_of((i * t2 + j) * pitch, V7X_SUBLANES)
            for sl in range(slabs):
                a_scr[sl, pl.ds(row0, 2 * n1), :] = a[:, sl * V7X_LANES:(sl + 1) * V7X_LANES]

    @pl.when(i >= n_stage1)
    def _stage2():
        for t in range(t1):
            k1 = (i - n_stage1) * t1 + t

            def over_s2(row):
                return jnp.concatenate(
                    [a_scr[sl, pl.ds(row, n2, stride=pitch), :] for sl in range(slabs)], axis=1)

            x = jnp.concatenate([over_s2(k1), over_s2(n1 + k1)], axis=0).astype(BF16)
            z = _dot(w_ref[...], x)
            zc = jnp.concatenate([z[:n2], z[n2:]], axis=1).astype(BF16)
            o_ref[0, :, t, :] = _dot(zc, cs_ref[...]) * scale


def _fourier_latent(ft, m1, w3, cs):
    b, n2, n1, fw = ft.shape
    s = n1 * n2
    t2, t1 = 32, 32
    n_stage1, n_stage2 = n2 // t2, n1 // t1
    tile1 = lambda i: jnp.minimum(i, n_stage1 - 1)
    pitch = 2 * n1 + V7X_SUBLANES
    y = pl.pallas_call(
        functools.partial(_fft_kernel, n_stage1=n_stage1, pitch=pitch,
                          scale=1.0 / math.sqrt(s * GROUP_DIM)),
        grid=(b, n_stage1 + n_stage2),
        in_specs=[pl.BlockSpec((1, t2, n1, fw), lambda bi, i: (bi, tile1(i), 0, 0)),
                  pl.BlockSpec((t2, 2 * n1, n1), lambda bi, i: (tile1(i), 0, 0)),
                  _resident(w3.shape), _resident(cs.shape)],
        out_specs=pl.BlockSpec((1, n2, t1, fw), lambda bi, i: (bi, 0, jnp.maximum(i - n_stage1, 0), 0)),
        out_shape=jax.ShapeDtypeStruct((b, n2, n1, fw), F32),
        scratch_shapes=[pltpu.VMEM((fw // V7X_LANES, n2 * pitch, V7X_LANES), F32)],
        compiler_params=_params("arbitrary", "arbitrary"),
        name="fft",
    )(ft, m1, w3, cs)
    return y.reshape(b, s, fw)


def _fourier_ctx_kernel(f_ref, cs_ref, dl_ref, o_ref, *, scale):
    fw = f_ref.shape[2]
    fc = _dot(f_ref[0], cs_ref[...])
    stacked = jnp.concatenate([fc[:, :fw], fc[:, fw:]], axis=0).astype(BF16)
    o_ref[0] = _dot(dl_ref[...], stacked) * scale


def _fourier_ctx(f, cs_t, dl):
    b, l, fw = f.shape
    return pl.pallas_call(
        functools.partial(_fourier_ctx_kernel, scale=1.0 / math.sqrt(l * GROUP_DIM)),
        grid=(b,),
        in_specs=[pl.BlockSpec((1, l, fw), lambda bi: (bi, 0, 0)),
                  _resident(cs_t.shape), _resident(dl.shape)],
        out_specs=pl.BlockSpec((1, l, fw), lambda bi: (bi, 0, 0)),
        out_shape=jax.ShapeDtypeStruct((b, l, fw), F32),
        compiler_params=_params("arbitrary"),
        name="fourier_ctx",
    )(f, cs_t, dl)


def _out_ff_kernel(x_ref, a_ref, s_ref, f_ref, g1_ref, sc2_ref, sh2_ref, g2_ref,
                   gmix_ref, gpm_ref, gpf_ref, gpo_ref, wo_ref, w1_ref, w2_ref, o_ref, *,
                   ff_chunk, row_chunks):
    aw = a_ref.shape[2]
    sw = s_ref.shape[2]
    gm = gmix_ref[...]
    d_ff = w1_ref.shape[1]
    n_rows = len(row_chunks)
    starts = [sum(row_chunks[:r]) for r in range(n_rows + 1)]
    gate1 = g1_ref[0, 0] * gpm_ref[...]
    gain2 = gpf_ref[...] * (1.0 + sc2_ref[0, 0])
    shift2 = sh2_ref[0, 0]
    gate2 = g2_ref[0, 0] * gpo_ref[...]

    def mix(r):
        rows = slice(starts[r], starts[r + 1])
        ycat = jnp.concatenate([
            (_rms(a_ref[0, rows, :].astype(F32)) * gm[:, :aw]).astype(BF16),
            (_rms(s_ref[0, rows, :]) * gm[:, aw:aw + sw]).astype(BF16),
            (_rms(f_ref[0, rows, :]) * gm[:, aw + sw:]).astype(BF16)], axis=1)
        y = _dot(ycat, wo_ref[...])
        x1 = x_ref[0, rows, :] + _rms(y) * gate1
        return x1, (_rms(x1) * gain2 + shift2).astype(BF16)

    n_ff = d_ff // ff_chunk

    def up(h2, c):
        hid = jnp.maximum(_dot(h2, w1_ref[:, c * ff_chunk:(c + 1) * ff_chunk]), 0.0)
        return (hid * hid).astype(BF16)

    def down(hid, c):
        return _dot(hid, w2_ref[c * ff_chunk:(c + 1) * ff_chunk, :])

    nxt = mix(0)
    for r in range(n_rows):
        x1, h2 = nxt
        hid = up(h2, 0)
        ff = None
        for c in range(n_ff):
            hid_next = up(h2, c + 1) if c + 1 < n_ff else None
            if c == 0 and r + 1 < n_rows:
                nxt = mix(r + 1)
            part = down(hid, c)
            ff = part if ff is None else ff + part
            hid = hid_next
        o_ref[0, starts[r]:starts[r + 1], :] = x1 + _rms(ff) * gate2


def _out_ff(x, a, sg, fo, mod, layer, mod_row, gmix, gpm, gpf, gpo, wo, w1, w2, tm):
    b, t, d = x.shape
    tok = lambda w: pl.BlockSpec((1, tm, w), lambda bi, i: (bi, i, 0))
    vec = lambda part: _mod_spec(d, layer, mod_row, part)
    row_chunks = (min(tm, 4 * BLOCK),) * (tm // min(tm, 4 * BLOCK))
    return pl.pallas_call(
        functools.partial(_out_ff_kernel, ff_chunk=min(w1.shape[1], 2048), row_chunks=row_chunks),
        grid=(b, t // tm),
        in_specs=[tok(d), tok(a.shape[2]), tok(sg.shape[2]), tok(fo.shape[2]),
                  vec(MOD_GATE1), vec(MOD_SCALE2), vec(MOD_SHIFT2), vec(MOD_GATE2),
                  _resident(gmix.shape), _resident(gpm.shape), _resident(gpf.shape), _resident(gpo.shape),
                  _resident(wo.shape), _resident(w1.shape), _resident(w2.shape)],
        out_specs=tok(d),
        out_shape=jax.ShapeDtypeStruct((b, t, d), F32),
        compiler_params=_params("arbitrary", "arbitrary"),
        name="out_ff",
    )(x, a, sg, fo, mod, mod, mod, mod, gmix, gpm, gpf, gpo, wo, w1, w2)


def _rope_tables(seq_len):
    rows = seq_len // GRID_W
    row = jnp.repeat(jnp.arange(rows), GRID_W).astype(F32)
    col = jnp.tile(jnp.arange(GRID_W), rows).astype(F32)
    n_freq = HEAD_DIM // 4
    inv_freq = ROPE_THETA ** (-jnp.arange(n_freq, dtype=F32) / n_freq)
    ang_r = row[:, None] * inv_freq[None, :]
    ang_c = col[:, None] * inv_freq[None, :]
    ang = jnp.concatenate([ang_r, ang_r, ang_c, ang_c], axis=-1)
    reps = V7X_LANES // HEAD_DIM
    cos = jnp.tile(jnp.cos(ang), (1, reps))
    sin = jnp.tile(jnp.sin(ang), (1, reps))
    even = (np.arange(V7X_LANES) // n_freq) % 2 == 0
    return cos, jnp.where(even, -sin, 0.0), jnp.where(even, 0.0, sin)


def kernel(x, c, ctx, c_ctx, w_ada, b_ada, g_pre_mix, w_in, sink, w_sgu, b_sgu, g_sgu,
           g_mix, w_out, g_post_mix, g_pre_ff, w_ff1, w_ff2, g_post_ff):
    b, s, d = x.shape
    ctx_len = ctx.shape[1]
    depth = w_ada.shape[0]
    n_q = sink.shape[1]
    aw = n_q * HEAD_DIM
    n_sgu = w_sgu.shape[1]
    sw = n_sgu * GROUP_DIM
    fw = g_mix.shape[1] - aw - sw
    kvw = (w_in.shape[2] - aw - 2 * sw - fw) // 2
    n_kv = kvw // HEAD_DIM
    grp = n_q // n_kv
    assert n_kv == 2 and kvw == V7X_LANES and s % (8 * BLOCK) == 0 and ctx_len % BLOCK == 0
    assert s % (V7X_SUBLANES * FFT_INNER) == 0 and fw % V7X_LANES == 0
    widths = (aw, kvw, sw, fw)

    rows = -(-(b + 1) // 8) * 8
    cc = jnp.zeros((rows, d), F32).at[:b].set(c).at[b].set(c_ctx)
    mod = _modulation(cc, w_ada, b_ada).reshape(depth, rows, 1, 6 * d)

    rope_x = _rope_tables(s)
    n_ctx = b * ctx_len
    ctx_tile = 8 * BLOCK if n_ctx % (8 * BLOCK) == 0 else ctx_len
    rope_c = (jnp.ones((n_ctx, V7X_LANES), F32), jnp.zeros((n_ctx, V7X_LANES), F32),
              jnp.zeros((n_ctx, V7X_LANES), F32))
    ones = _mxu_const(np.kron(np.eye(n_sgu), np.ones((GROUP_DIM, GROUP_DIM))))
    m1, w3 = _dft_tables(s, FFT_INNER)
    cs_np = _channel_dft(fw)
    cs = _mxu_const(cs_np)
    cs_t = _mxu_const(np.concatenate([cs_np[:fw], cs_np[fw:]], axis=1))
    kl = np.arange(ctx_len)
    al = 2.0 * np.pi * ((kl[:, None] * kl[None, :]) % ctx_len) / ctx_len
    dl = _mxu_const(np.concatenate([np.cos(al), -np.sin(al)], axis=1))

    lane = np.arange(aw)
    perm = np.where(lane % V7X_LANES < HEAD_DIM,
                    (lane // V7X_LANES) * HEAD_DIM + lane % V7X_LANES,
                    (grp + lane // V7X_LANES) * HEAD_DIM + lane % V7X_LANES - HEAD_DIM)

    row2 = lambda v: v.reshape(1, -1)
    h_ctx = ctx.reshape(1, n_ctx, d)
    for l in range(depth):
        last = l == depth - 1
        wsgu = w_sgu[l].reshape(n_sgu * CHUNK, CHUNK).astype(BF16)
        bsgu = jnp.repeat(b_sgu[l].T, GROUP_DIM, axis=1)
        gsgu = g_sgu[l].reshape(1, sw)
        gmix = row2(jnp.concatenate([g_mix[l][:aw][perm], g_mix[l][aw:]]))
        wo = jnp.concatenate([w_out[l][:aw][perm], w_out[l][aw:]], axis=0).astype(BF16)
        shared = (row2(g_pre_mix[l]), w_in[l].astype(BF16))
        sgu_w = (wsgu, bsgu, gsgu, ones)
        ff_w = (gmix, row2(g_post_mix[l]), row2(g_pre_ff[l]), row2(g_post_ff[l]), wo,
                w_ff1[l].astype(BF16), w_ff2[l].astype(BF16))

        cq, ckd, cvt, csg, cf = _in_proj(h_ctx, mod, l, b, *shared, rope_c, *sgu_w, ctx_tile, widths,
                                         fft_layout=False)
        ckd = ckd.reshape(b, ctx_len, 2 * kvw)
        if not last:
            ca = _attention(sink[l], cq.reshape(b, ctx_len, aw), None, None, ckd, cvt, ctx_len, local=False)
            cfo = _fourier_ctx(cf.reshape(b, ctx_len, fw), cs_t, dl)
            h_ctx = _out_ff(h_ctx, ca.reshape(1, b * ctx_len, aw), csg, cfo.reshape(1, b * ctx_len, fw),
                            mod, l, b, *ff_w, ctx_tile)

        q, kd, vt, sg, f = _in_proj(x, mod, l, None, *shared, rope_x, *sgu_w, 8 * BLOCK, widths,
                                    fft_layout=True)
        a = _attention(sink[l], q, kd, vt, ckd, cvt, 16 * BLOCK, local=True)
        fo = _fourier_latent(f, m1, w3, cs)
        x = _out_ff(x, a, sg, fo, mod, l, None, *ff_w, 8 * BLOCK)
    return x
```
